```python
import jax
import jax.numpy as jnp
from jax import lax
import numpy as np


D_MODEL = 1024
BATCH = 1
SEQ = 16384
DEPTH = 2
DEC_BATCH = 8
DEC_SEQ = 32
PAST_LEN = 2048

CHUNK = 64
MIX_BRANCH = D_MODEL // 2
POOL_WINDOWS = (2, 4, 8, 16)
POOL_GROUPS = 4
POOL_WIDTH = MIX_BRANCH
POOL_GW = POOL_WIDTH // POOL_GROUPS
POOL_HIST = 15
GLA_HEADS = 4
GLA_QK_W = D_MODEL // 4
GLA_V_W = MIX_BRANCH
GLA_DK = GLA_QK_W // GLA_HEADS
GLA_DV = GLA_V_W // GLA_HEADS
GLA_RANK = 16
GLA_TAU = 16.0
ATT_HEADS = 8
ATT_DH = 64
ATT_WIDTH = ATT_HEADS * ATT_DH
LEFT_CHUNKS = 8
ATT_WIN = LEFT_CHUNKS * CHUNK
MAX_REL = 128
N_BRANCH = 3
D_FF = -(-8 * D_MODEL // (3 * 256)) * 256
D_IN = POOL_WIDTH + 2 * GLA_QK_W + 2 * GLA_V_W + GLA_RANK + 3 * ATT_WIDTH + N_BRANCH * D_MODEL
EPS = 1e-6

kernel_name = "hybrid_pool_gla_chunkattn_stream_step"


def rms_norm(x, g):
    xf = x.astype(jnp.float32)
    y = xf * lax.rsqrt(jnp.mean(xf * xf, axis=-1, keepdims=True) + EPS)
    return (y * g.astype(jnp.float32)).astype(x.dtype)


def in_proj_offsets():
    sizes = (POOL_WIDTH, GLA_QK_W, GLA_QK_W, GLA_V_W, GLA_V_W, GLA_RANK,
             ATT_WIDTH, ATT_WIDTH, ATT_WIDTH, N_BRANCH * D_MODEL)
    return [int(o) for o in np.cumsum(sizes)[:-1]]


def pool_mixer(u, hist, pos0, pool_map, pool_scale):
    B, T, _ = u.shape
    f32 = jnp.float32
    full = jnp.concatenate([hist.astype(u.dtype), u], axis=1)
    ff = full.astype(f32)
    cs = jnp.concatenate([jnp.zeros((B, 1, POOL_WIDTH), f32), jnp.cumsum(ff, axis=1)], axis=1)
    pos = (pos0 + jnp.arange(T)).astype(f32)
    end = cs[:, POOL_HIST + 1:POOL_HIST + 1 + T]
    groups = []
    for g, w in enumerate(POOL_WINDOWS):
        sl = slice(g * POOL_GW, (g + 1) * POOL_GW)
        start = cs[:, POOL_HIST + 1 - w:POOL_HIST + 1 - w + T, sl]
        cnt = jnp.minimum(float(w), pos + 1.0)[None, :, None]
        groups.append((end[..., sl] - start) / cnt - ff[:, POOL_HIST:, sl])
    p = jnp.stack(groups, axis=2)
    y = jnp.einsum('btgc,gcd->btgd', p, pool_map.astype(f32)).reshape(B, T, POOL_WIDTH)
    y = y * pool_scale.astype(f32)
    return y.astype(u.dtype), full[:, -POOL_HIST:]


def gla_recurrence(q, k, v, log_a, s0):
    B, T, H, DK = q.shape
    DV = v.shape[-1]
    L = min(CHUNK, T)
    n = T // L

    def blocks(a):
        return a.reshape(B, n, L, H, a.shape[-1]).transpose(1, 0, 3, 2, 4)

    causal = jnp.tril(jnp.ones((L, L), dtype=bool))[:, :, None]

    def step(s, blk):
        qb, kb, vb, gb = blk
        b = jnp.cumsum(gb, axis=2)
        o_inter = jnp.einsum('bhld,bhde->bhle', qb * jnp.exp(b), s)
        diff = b[:, :, :, None, :] - b[:, :, None, :, :]
        decay = jnp.exp(jnp.where(causal, diff, -jnp.inf))
        att = jnp.einsum('bhid,bhjd,bhijd->bhij', qb, kb, decay)
        o = o_inter + jnp.einsum('bhij,bhje->bhie', att, vb)
        b_last = b[:, :, -1:, :]
        s_new = jnp.exp(b_last[:, :, 0, :, None]) * s + jnp.einsum('bhld,bhle->bhde', kb * jnp.exp(b_last - b), vb)
        return s_new, o

    s_fin, o = lax.scan(step, s0, (blocks(q), blocks(k), blocks(v), blocks(log_a)))
    o = o.transpose(1, 0, 3, 2, 4).reshape(B, T, H, DV)
    return o, s_fin


def rel_bias_matrix(rel_bias, q_rel, k_rel):
    dist = q_rel[:, None] - k_rel[None, :]
    idx = jnp.clip(dist, -MAX_REL, MAX_REL) + MAX_REL
    return rel_bias[:, idx].astype(jnp.float32)


def band_softmax_attention(qb, kb, vb, bias, valid):
    s = jnp.einsum('bnqhd,bnkhd->bnhqk', qb, kb).astype(jnp.float32) * (ATT_DH ** -0.5) + bias[None, None]
    s = jnp.where(valid[None, :, None, None, :], s, -jnp.inf)
    p = jax.nn.softmax(s, axis=-1).astype(vb.dtype)
    return jnp.einsum('bnhqk,bnkhd->bnqhd', p, vb)


def chunk_band_prompt(q, k, v, rel_bias):
    B, T, H, DH = q.shape
    n = T // CHUNK

    def band(a):
        ap = jnp.pad(a, ((0, 0), (ATT_WIN, 0), (0, 0), (0, 0))).reshape(B, n + LEFT_CHUNKS, CHUNK, H, DH)
        return jnp.concatenate([ap[:, c:c + n] for c in range(LEFT_CHUNKS + 1)], axis=2)

    q_rel = jnp.arange(CHUNK)
    k_rel = jnp.arange((LEFT_CHUNKS + 1) * CHUNK) - ATT_WIN
    valid = (jnp.arange(n)[:, None] * CHUNK + k_rel[None, :]) >= 0
    o = band_softmax_attention(q.reshape(B, n, CHUNK, H, DH), band(k), band(v),
                               rel_bias_matrix(rel_bias, q_rel, k_rel), valid)
    return o.reshape(B, T, H * DH)


def chunk_band_sample(q, k, v, k_cache, v_cache, rel_bias):
    B, T, H, DH = q.shape
    W = k_cache.shape[1]
    keys = jnp.concatenate([k_cache.astype(k.dtype), k], axis=1)[:, None]
    vals = jnp.concatenate([v_cache.astype(v.dtype), v], axis=1)[:, None]
    q_rel = jnp.arange(T)
    k_rel = jnp.concatenate([jnp.arange(W) - W, jnp.arange(T)])
    valid = jnp.ones((1, W + T), dtype=bool)
    o = band_softmax_attention(q[:, None], keys, vals, rel_bias_matrix(rel_bias, q_rel, k_rel), valid)
    return o.reshape(B, T, H * DH)


def hybrid_layer(x, state, pos0, lw):
    (attn_norm_g, w_in, w_gate2, b_gate, gla_norm_g, pool_map, pool_scale, rel_bias,
     w_branch, w_out, ffn_norm_g, w_ffn_in, w_ffn_out) = lw
    B, T, _ = x.shape
    f32 = jnp.float32
    xn = rms_norm(x, attn_norm_g)
    h = xn @ w_in
    u_a, q_b, k_b, v_b, r_b, z_b, q_c, k_c, v_c, gate_logits = jnp.split(h, in_proj_offsets(), axis=-1)

    if state is None:
        pool_hist = jnp.zeros((B, POOL_HIST, POOL_WIDTH), x.dtype)
        s0 = jnp.zeros((B, GLA_HEADS, GLA_DK, GLA_DV), f32)
    else:
        pool_hist, s0, k_cache, v_cache = state
        s0 = s0.astype(f32)

    y_a, pool_new = pool_mixer(u_a, pool_hist, pos0, pool_map, pool_scale)

    log_a = jax.nn.log_sigmoid((z_b @ w_gate2 + b_gate).astype(f32)) / GLA_TAU
    qh = q_b.astype(f32).reshape(B, T, GLA_HEADS, GLA_DK) * (GLA_DK ** -0.5)
    kh = k_b.astype(f32).reshape(B, T, GLA_HEADS, GLA_DK)
    vh = v_b.astype(f32).reshape(B, T, GLA_HEADS, GLA_DV)
    o_b, s_fin = gla_recurrence(qh, kh, vh, log_a.reshape(B, T, GLA_HEADS, GLA_DK), s0)
    o_b = o_b * lax.rsqrt(jnp.mean(o_b * o_b, axis=-1, keepdims=True) + EPS) * gla_norm_g.astype(f32)
    y_b = (o_b.reshape(B, T, GLA_V_W) * jax.nn.silu(r_b.astype(f32))).astype(x.dtype)

    qa = q_c.reshape(B, T, ATT_HEADS, ATT_DH)
    ka = k_c.reshape(B, T, ATT_HEADS, ATT_DH)
    va = v_c.reshape(B, T, ATT_HEADS, ATT_DH)
    if state is None:
        y_c = chunk_band_prompt(qa, ka, va, rel_bias)
        keep = min(ATT_WIN, T)
        k_new, v_new = ka[:, T - keep:], va[:, T - keep:]
    else:
        y_c = chunk_band_sample(qa, ka, va, k_cache, v_cache, rel_bias)
        k_new, v_new = ka, va

    ys = jnp.stack([y_a, y_b, y_c.astype(x.dtype)], axis=2)
    branch = jnp.einsum('btgc,gcd->btgd', ys, w_branch)
    gates = jax.nn.sigmoid(gate_logits.reshape(B, T, N_BRANCH, D_MODEL))
    x = x + jnp.sum(gates * branch, axis=2) @ w_out

    hf = rms_norm(x, ffn_norm_g) @ w_ffn_in
    a, g = jnp.split(hf, 2, axis=-1)
    x = x + (jax.nn.silu(a) * g) @ w_ffn_out
    return x, (pool_new, s_fin.astype(x.dtype), k_new, v_new)


def setup_inputs(seed: int = 0) -> dict:
    key = jax.random.key(seed)
    ks = jax.random.split(key, 20)
    f32 = jnp.float32

    def nrm(k, shape, scale):
        return jax.random.normal(k, shape, f32) * scale

    c_win = min(ATT_WIN, PAST_LEN)
    return {
        'x_prompt': nrm(ks[0], (BATCH, SEQ, D_MODEL), 1.0),
        'x_sample': nrm(ks[1], (DEC_BATCH, DEC_SEQ, D_MODEL), 1.0),
        'cache_pool': nrm(ks[2], (DEPTH, DEC_BATCH, POOL_HIST, POOL_WIDTH), 1.0),
        'state_gla': nrm(ks[3], (DEPTH, DEC_BATCH, GLA_HEADS, GLA_DK, GLA_DV), 1.0),
        'cache_k': nrm(ks[4], (DEPTH, DEC_BATCH, c_win, ATT_HEADS, ATT_DH), 1.0),
        'cache_v': nrm(ks[5], (DEPTH, DEC_BATCH, c_win, ATT_HEADS, ATT_DH), 1.0),
        'attn_norm_g': 1.0 + nrm(ks[6], (DEPTH, D_MODEL), 0.1),
        'w_in': nrm(ks[7], (DEPTH, D_MODEL, D_IN), D_MODEL ** -0.5),
        'w_gate2': nrm(ks[8], (DEPTH, GLA_RANK, GLA_QK_W), GLA_RANK ** -0.5),
        'b_gate': nrm(ks[9], (DEPTH, GLA_QK_W), 0.1),
        'gla_norm_g': 1.0 + nrm(ks[10], (DEPTH, GLA_DV), 0.1),
        'pool_map': nrm(ks[11], (DEPTH, POOL_GROUPS, POOL_GW, POOL_GW), POOL_GW ** -0.5),
        'pool_scale': 1.0 + nrm(ks[12], (DEPTH, POOL_WIDTH), 0.1),
        'rel_bias': nrm(ks[13], (DEPTH, ATT_HEADS, 2 * MAX_REL + 1), 0.5),
        'w_branch': nrm(ks[14], (DEPTH, N_BRANCH, MIX_BRANCH, D_MODEL), MIX_BRANCH ** -0.5),
        'w_out': nrm(ks[15], (DEPTH, D_MODEL, D_MODEL), D_MODEL ** -0.5),
        'ffn_norm_g': 1.0 + nrm(ks[16], (DEPTH, D_MODEL), 0.1),
        'w_ffn_in': nrm(ks[17], (DEPTH, D_MODEL, 2 * D_FF), D_MODEL ** -0.5),
        'w_ffn_out': nrm(ks[18], (DEPTH, D_FF, D_MODEL), D_FF ** -0.5),
        'final_norm_g': 1.0 + nrm(ks[19], (D_MODEL,), 0.1),
    }


def reference(x_prompt, x_sample, cache_pool, state_gla, cache_k, cache_v, attn_norm_g, w_in, w_gate2, b_gate,
              gla_norm_g, pool_map, pool_scale, rel_bias, w_branch, w_out, ffn_norm_g, w_ffn_in, w_ffn_out,
              final_norm_g):
    hp, hs = x_prompt, x_sample
    pool_p, pool_s, gla_p, gla_s, k_p, k_s, v_p, v_s = [], [], [], [], [], [], [], []
    for l in range(DEPTH):
        lw = (attn_norm_g[l], w_in[l], w_gate2[l], b_gate[l], gla_norm_g[l], pool_map[l], pool_scale[l],
              rel_bias[l], w_branch[l], w_out[l], ffn_norm_g[l], w_ffn_in[l], w_ffn_out[l])
        hp, (pp, gp, kp, vp) = hybrid_layer(hp, None, 0, lw)
        hs, (ps, gs, kss, vss) = hybrid_layer(hs, (cache_pool[l], state_gla[l], cache_k[l], cache_v[l]), PAST_LEN, lw)
        pool_p.append(pp); gla_p.append(gp); k_p.append(kp); v_p.append(vp)
        pool_s.append(ps); gla_s.append(gs); k_s.append(kss); v_s.append(vss)
    y_prompt = rms_norm(hp, final_norm_g)
    y_sample = rms_norm(hs, final_norm_g)
    new_pool_prompt = jnp.stack(pool_p)
    new_pool_sample = jnp.stack(pool_s)
    new_gla_prompt = jnp.stack(gla_p)
    new_gla_sample = jnp.stack(gla_s)
    new_k_prompt = jnp.stack(k_p)
    new_k_sample = jnp.stack(k_s)
    new_v_prompt = jnp.stack(v_p)
    new_v_sample = jnp.stack(v_s)
    return (y_prompt, y_sample, new_pool_prompt, new_pool_sample, new_gla_prompt, new_gla_sample,
            new_k_prompt, new_k_sample, new_v_prompt, new_v_sample)
```

```python
import functools

import jax
import jax.numpy as jnp
from jax import lax
from jax.experimental import pallas as pl
from jax.experimental.pallas import tpu as pltpu

F32 = jnp.float32
BF16 = jnp.bfloat16

D_MODEL = 1024
PAST_LEN = 2048
CHUNK = 64
MIX = 512
POOL_WINDOWS = (2, 4, 8, 16)
POOL_GW = 128
POOL_HIST = 15
HIST_ROWS = 16
GLA_HEADS = 4
GLA_DK = 64
GLA_DV = 128
GLA_QK_W = 256
GLA_RANK = 16
GLA_TAU = 16.0
ATT_HEADS = 8
ATT_DH = 64
ATT_WIN = 512
MAX_REL = 128
N_BRANCH = 3
D_FF = 2816
EPS = 1e-6
LANES = 128

MIX_U, MIX_VB, MIX_RB, MIX_QC, MIX_KC, MIX_VC = 0, 512, 1024, 1536, 2048, 2560
MIX_W = 3072
GLA_Q, GLA_K, GLA_Z = 0, 256, 512
GLA_W = 640
GATE_W = N_BRANCH * D_MODEL
YS_A, YS_B, YS_C = 0, 512, 1024

ROW_TILE = 256
SEQ_TILE = 512
VMEM_LIMIT = 56 * 1024 * 1024

NT_DIMS = (((1,), (1,)), ((), ()))
TN_DIMS = (((0,), (0,)), ((), ()))


def _sigmoid(x):
    return 1.0 / (1.0 + jnp.exp(-x))


def _rms(x, g):
    return x * lax.rsqrt(jnp.mean(x * x, axis=-1, keepdims=True) + EPS) * g


def _const_spec(shape):
    nd = len(shape)
    return pl.BlockSpec(shape, lambda *_: (0,) * nd, pipeline_mode=pl.Buffered(1))


def _inproj_kernel(x_ref, g_ref, wg_ref, wm_ref, wl_ref, og_ref, om_ref, ol_ref):
    xb = _rms(x_ref[...], g_ref[...]).astype(BF16)
    og_ref[...] = jnp.dot(xb, wg_ref[...], preferred_element_type=F32)
    om_ref[...] = jnp.dot(xb, wm_ref[...], preferred_element_type=F32)
    ol_ref[...] = jnp.dot(xb, wl_ref[...], preferred_element_type=F32)


def _inproj(x, g, w_gates, w_mix, w_gla):
    rows = x.shape[0]
    tm = min(ROW_TILE, rows)
    row_spec = lambda w: pl.BlockSpec((tm, w), lambda i: (i, 0))
    return pl.pallas_call(
        _inproj_kernel,
        grid=(rows // tm,),
        in_specs=[row_spec(D_MODEL), _const_spec((1, D_MODEL)), _const_spec(w_gates.shape),
                  _const_spec(w_mix.shape), _const_spec(w_gla.shape)],
        out_specs=[row_spec(GATE_W), row_spec(MIX_W), row_spec(GLA_W)],
        out_shape=[jax.ShapeDtypeStruct((rows, GATE_W), F32), jax.ShapeDtypeStruct((rows, MIX_W), F32),
                   jax.ShapeDtypeStruct((rows, GLA_W), F32)],
        compiler_params=pltpu.CompilerParams(dimension_semantics=("arbitrary",), vmem_limit_bytes=VMEM_LIMIT),
        name="in_proj",
    )(x, g, w_gates, w_mix, w_gla)


def _mixer_kernel(tt, lg, lq, has_state, pos0, *refs):
    refs = list(refs)
    mix_ref, gla_ref = refs[:2]
    k = 2
    if has_state:
        hist_ref, s0_ref, kc_ref, vc_ref = refs[k:k + 4]
        k += 4
    wg2_ref, bg_ref, gng_ref, pmap_ref, pscale_ref, bias_ref = refs[k:k + 6]
    k += 6
    ys_ref, sfin_ref = refs[k:k + 2]
    carry_ref, st_ref, kbuf_ref, vbuf_ref, b_ref = refs[k + 2:]

    i = pl.program_id(1)
    wk = ATT_WIN + lq
    lane = lax.broadcasted_iota(jnp.int32, (1, LANES), 1)
    head_mask = (lane < ATT_DH, lane >= ATT_DH)

    @pl.when(i == 0)
    def _init():
        if has_state:
            carry_ref[...] = hist_ref[0]
            st_ref[...] = s0_ref[0]
            kbuf_ref[0:ATT_WIN, :] = kc_ref[0].astype(BF16)
            vbuf_ref[0:ATT_WIN, :] = vc_ref[0].astype(BF16)
        else:
            carry_ref[...] = jnp.zeros_like(carry_ref)
            st_ref[...] = jnp.zeros_like(st_ref)
            kbuf_ref[0:ATT_WIN, :] = jnp.zeros((ATT_WIN, MIX), BF16)
            vbuf_ref[0:ATT_WIN, :] = jnp.zeros((ATT_WIN, MIX), BF16)

    u = mix_ref[:, MIX_U:MIX_U + MIX]
    ext = jnp.concatenate([carry_ref[...], u], axis=0)
    carry_ref[...] = u[tt - HIST_ROWS:, :]
    row = lax.broadcasted_iota(jnp.int32, (tt, POOL_GW), 0)
    n_seen = (pos0 + i * tt + row + 1).astype(F32)
    for g, w in enumerate(POOL_WINDOWS):
        cols = slice(g * POOL_GW, (g + 1) * POOL_GW)
        s = ext[:, cols]
        shift = 1
        while shift < w:
            s = s + pltpu.roll(s, shift, 0)
            shift *= 2
        p = s[HIST_ROWS:, :] / jnp.minimum(float(w), n_seen) - u[:, cols]
        y = jnp.dot(p.astype(BF16), pmap_ref[g], preferred_element_type=F32) * pscale_ref[:, cols]
        ys_ref[:, YS_A + g * POOL_GW:YS_A + (g + 1) * POOL_GW] = y.astype(BF16)

    z = gla_ref[:, GLA_Z:GLA_Z + LANES].astype(BF16)
    pre = jnp.dot(z, wg2_ref[...], preferred_element_type=F32) + bg_ref[...]
    log_a = -(jnp.maximum(-pre, 0.0) + jnp.log1p(jnp.exp(-jnp.abs(pre)))) / GLA_TAU
    row_in_blk = lax.broadcasted_iota(jnp.int32, (tt, GLA_QK_W), 0) & (lg - 1)
    b = log_a
    shift = 1
    while shift < lg:
        b = b + jnp.where(row_in_blk >= shift, pltpu.roll(b, shift, 0), 0.0)
        shift *= 2
    b_ref[...] = b

    ri = lax.broadcasted_iota(jnp.int32, (lg, lg), 0)
    ci = lax.broadcasted_iota(jnp.int32, (lg, lg), 1)
    causal = ci <= ri

    def gla_block(c, carry):
        r0 = pl.multiple_of(c * lg, lg)
        rows = pl.ds(r0, lg)
        for p in range(GLA_HEADS // 2):
            bq = b_ref[rows, p * LANES:(p + 1) * LANES]
            eb = jnp.exp(bq)
            enb = jnp.exp(-bq)
            e_last = eb[lg - 1:lg, :]
            qs = gla_ref[rows, GLA_Q + p * LANES:GLA_Q + (p + 1) * LANES] * (GLA_DK ** -0.5) * eb
            kt = gla_ref[rows, GLA_K + p * LANES:GLA_K + (p + 1) * LANES] * enb
            ktb = kt.astype(BF16)
            kdb = (kt * e_last).astype(BF16)
            st = st_ref[p]
            stb = st.astype(BF16)
            upd = []
            for hl in range(2):
                h = 2 * p + hl
                qm = jnp.where(head_mask[hl], qs, 0.0).astype(BF16)
                att = lax.dot_general(qm, ktb, NT_DIMS, preferred_element_type=F32)
                att = jnp.where(causal, att, 0.0).astype(BF16)
                vh = mix_ref[rows, MIX_VB + h * GLA_DV:MIX_VB + (h + 1) * GLA_DV].astype(BF16)
                o = (jnp.dot(att, vh, preferred_element_type=F32)
                     + lax.dot_general(qm, stb, NT_DIMS, preferred_element_type=F32))
                o = _rms(o, gng_ref[...])
                r = mix_ref[rows, MIX_RB + h * GLA_DV:MIX_RB + (h + 1) * GLA_DV]
                ys_ref[rows, YS_B + h * GLA_DV:YS_B + (h + 1) * GLA_DV] = (o * (r * _sigmoid(r))).astype(BF16)
                upd.append(lax.dot_general(vh, kdb, TN_DIMS, preferred_element_type=F32))
            st_ref[p] = e_last * st + jnp.where(head_mask[0], upd[0], upd[1])
        return carry

    lax.fori_loop(0, tt // lg, gla_block, 0)

    kbuf_ref[ATT_WIN:ATT_WIN + tt, :] = mix_ref[:, MIX_KC:MIX_KC + MIX].astype(BF16)
    vbuf_ref[ATT_WIN:ATT_WIN + tt, :] = mix_ref[:, MIX_VC:MIX_VC + MIX].astype(BF16)
    kcol = lax.broadcasted_iota(jnp.int32, (1, wk), 1)
    first_tile_neg = jnp.where(i == 0, -jnp.inf, 0.0).astype(F32)

    def att_chunk(c, carry):
        r0 = pl.multiple_of(c * lq, lq)
        qrows = pl.ds(r0, lq)
        krows = pl.ds(r0, wk)
        if not has_state:
            invalid = jnp.where(kcol < ATT_WIN - c * lq, first_tile_neg, 0.0)
        for p in range(ATT_HEADS // 2):
            cols = slice(p * LANES, (p + 1) * LANES)
            q2 = mix_ref[qrows, MIX_QC + p * LANES:MIX_QC + (p + 1) * LANES] * (ATT_DH ** -0.5)
            k2 = kbuf_ref[krows, cols]
            v2 = vbuf_ref[krows, cols]
            outs = []
            for hl in range(2):
                qm = jnp.where(head_mask[hl], q2, 0.0).astype(BF16)
                s = lax.dot_general(qm, k2, NT_DIMS, preferred_element_type=F32) + bias_ref[2 * p + hl]
                if not has_state:
                    s = s + invalid
                e = jnp.exp(s - jnp.max(s, axis=-1, keepdims=True))
                den = jnp.sum(e, axis=-1, keepdims=True)
                outs.append(jnp.dot(e.astype(BF16), v2, preferred_element_type=F32) / den)
            ys_ref[qrows, YS_C + p * LANES:YS_C + (p + 1) * LANES] = (
                jnp.where(head_mask[0], outs[0], outs[1]).astype(BF16))
        return carry

    lax.fori_loop(0, tt // lq, att_chunk, 0)

    if not has_state:
        kbuf_ref[0:ATT_WIN, :] = kbuf_ref[tt:tt + ATT_WIN, :]
        vbuf_ref[0:ATT_WIN, :] = vbuf_ref[tt:tt + ATT_WIN, :]

    @pl.when(i == pl.num_programs(1) - 1)
    def _fin():
        sfin_ref[0] = st_ref[...]


def _mixers(mix, gla, state, lw, n_seq, seq_len, pos0):
    has_state = state is not None
    tt = min(SEQ_TILE, seq_len)
    lg = min(CHUNK, seq_len)
    lq = min(CHUNK, seq_len)
    nt = seq_len // tt
    wk = ATT_WIN + lq
    assert has_state or tt == ATT_WIN
    kernel = functools.partial(_mixer_kernel, tt, lg, lq, has_state, pos0)

    tile = lambda w: pl.BlockSpec((tt, w), lambda s, i: (s * nt + i, 0))
    in_specs = [tile(MIX_W), tile(GLA_W)]
    args = [mix, gla]
    if has_state:
        hist, s0t, kc, vc = state
        in_specs += [pl.BlockSpec((1, HIST_ROWS, MIX), lambda s, i: (s, 0, 0)),
                     pl.BlockSpec((1, 2, LANES, LANES), lambda s, i: (s, 0, 0, 0)),
                     pl.BlockSpec((1, ATT_WIN, MIX), lambda s, i: (s, 0, 0)),
                     pl.BlockSpec((1, ATT_WIN, MIX), lambda s, i: (s, 0, 0))]
        args += [hist, s0t, kc, vc]
    consts = [lw["w_gate2"], lw["b_gate"], lw["gla_norm_g"], lw["pool_map"], lw["pool_scale"],
              lw["bias_s"] if has_state else lw["bias_p"]]
    in_specs += [_const_spec(c.shape) for c in consts]
    args += consts

    return pl.pallas_call(
        kernel,
        grid=(n_seq, nt),
        in_specs=in_specs,
        out_specs=[pl.BlockSpec((tt, 3 * MIX), lambda s, i: (s * nt + i, 0)),
                   pl.BlockSpec((1, 2, LANES, LANES), lambda s, i: (s, 0, 0, 0))],
        out_shape=[jax.ShapeDtypeStruct((n_seq * seq_len, 3 * MIX), BF16),
                   jax.ShapeDtypeStruct((n_seq, 2, LANES, LANES), F32)],
        scratch_shapes=[pltpu.VMEM((HIST_ROWS, MIX), F32),
                        pltpu.VMEM((2, LANES, LANES), F32),
                        pltpu.VMEM((ATT_WIN + tt, MIX), BF16),
                        pltpu.VMEM((ATT_WIN + tt, MIX), BF16),
                        pltpu.VMEM((tt, GLA_QK_W), F32)],
        compiler_params=pltpu.CompilerParams(dimension_semantics=("arbitrary", "arbitrary"),
                                             vmem_limit_bytes=VMEM_LIMIT),
        name="mixers_sample" if has_state else "mixers_prompt",
    )(*args)


def _merge_ffn_kernel(final, x_ref, ys_ref, hg_ref, wbr_ref, wout_ref, g2_ref, wa_ref, wgt_ref, wo_ref, fg_ref,
                      out_ref):
    merged = None
    for g in range(N_BRANCH):
        branch = jnp.dot(ys_ref[:, g * MIX:(g + 1) * MIX], wbr_ref[g], preferred_element_type=F32)
        term = _sigmoid(hg_ref[:, g * D_MODEL:(g + 1) * D_MODEL]) * branch
        merged = term if merged is None else merged + term
    x1 = x_ref[...] + jnp.dot(merged.astype(BF16), wout_ref[...], preferred_element_type=F32)
    xn = _rms(x1, g2_ref[...]).astype(BF16)
    a = jnp.dot(xn, wa_ref[...], preferred_element_type=F32)
    gt = jnp.dot(xn, wgt_ref[...], preferred_element_type=F32)
    act = (a * _sigmoid(a) * gt).astype(BF16)
    x2 = x1 + jnp.dot(act, wo_ref[...], preferred_element_type=F32)
    if final:
        x2 = _rms(x2, fg_ref[...])
    out_ref[...] = x2


def _merge_ffn(x, ys, gates, lw, final_g):
    rows = x.shape[0]
    tm = min(ROW_TILE, rows)
    final = final_g is not None
    fg = final_g if final else lw["ffn_norm_g"]
    row_spec = lambda w: pl.BlockSpec((tm, w), lambda i: (i, 0))
    consts = [lw["w_branch"], lw["w_out"], lw["ffn_norm_g"], lw["w_ffn_a"], lw["w_ffn_g"], lw["w_ffn_out"], fg]
    return pl.pallas_call(
        functools.partial(_merge_ffn_kernel, final),
        grid=(rows // tm,),
        in_specs=[row_spec(D_MODEL), row_spec(3 * MIX), row_spec(GATE_W)] + [_const_spec(c.shape) for c in consts],
        out_specs=row_spec(D_MODEL),
        out_shape=jax.ShapeDtypeStruct((rows, D_MODEL), F32),
        compiler_params=pltpu.CompilerParams(dimension_semantics=("arbitrary",), vmem_limit_bytes=VMEM_LIMIT),
        name="merge_ffn",
    )(x, ys, gates, *consts)


def _rel_bias_matrix(rel_bias, lq):
    q_rel = jnp.arange(lq)
    k_rel = jnp.arange(ATT_WIN + lq) - ATT_WIN
    idx = jnp.clip(q_rel[:, None] - k_rel[None, :], -MAX_REL, MAX_REL) + MAX_REL
    return rel_bias[:, idx].astype(F32)


def _layer_weights(l, attn_norm_g, w_in, w_gate2, b_gate, gla_norm_g, pool_map, pool_scale, rel_bias, w_branch,
                   w_out, ffn_norm_g, w_ffn_in, w_ffn_out, sample_len):
    o = [0, 512, 768, 1024, 1536, 2048, 2064, 2576, 3088, 3600, 6672]
    w = w_in[l]
    col = lambda a, b: w[:, o[a]:o[b]]
    w_mix = jnp.concatenate([col(0, 1), col(3, 4), col(4, 5), col(6, 7), col(7, 8), col(8, 9)], axis=1)
    w_gla = jnp.concatenate([col(1, 2), col(2, 3), col(5, 6),
                             jnp.zeros((D_MODEL, LANES - GLA_RANK), w.dtype)], axis=1)
    wg2 = jnp.concatenate([w_gate2[l], jnp.zeros((LANES - GLA_RANK, GLA_QK_W), w_gate2.dtype)], axis=0)
    return dict(
        attn_norm_g=attn_norm_g[l][None], w_gates=col(9, 10).astype(BF16), w_mix=w_mix.astype(BF16),
        w_gla=w_gla.astype(BF16), w_gate2=wg2.astype(BF16), b_gate=b_gate[l][None],
        gla_norm_g=gla_norm_g[l][None], pool_map=pool_map[l].astype(BF16), pool_scale=pool_scale[l][None],
        bias_p=_rel_bias_matrix(rel_bias[l], CHUNK), bias_s=_rel_bias_matrix(rel_bias[l], sample_len),
        w_branch=w_branch[l].astype(BF16), w_out=w_out[l].astype(BF16), ffn_norm_g=ffn_norm_g[l][None],
        w_ffn_a=w_ffn_in[l][:, :D_FF].astype(BF16), w_ffn_g=w_ffn_in[l][:, D_FF:].astype(BF16),
        w_ffn_out=w_ffn_out[l].astype(BF16))


def _layer(x, n_seq, seq_len, state, lw, pos0, final_g):
    gates, mix, gla = _inproj(x, lw["attn_norm_g"], lw["w_gates"], lw["w_mix"], lw["w_gla"])
    if state is not None:
        hist, s0, kc, vc = state
        hist = jnp.pad(hist, ((0, 0), (HIST_ROWS - POOL_HIST, 0), (0, 0)))
        s0t = s0.reshape(n_seq, 2, 2, GLA_DK, GLA_DV).transpose(0, 1, 4, 2, 3).reshape(n_seq, 2, LANES, LANES)
        state = (hist, s0t, kc.reshape(n_seq, ATT_WIN, MIX), vc.reshape(n_seq, ATT_WIN, MIX))
    ys, s_fin_t = _mixers(mix, gla, state, lw, n_seq, seq_len, pos0)
    x_new = _merge_ffn(x, ys, gates, lw, final_g)

    mix3 = mix.reshape(n_seq, seq_len, MIX_W)
    keep = min(ATT_WIN, seq_len)
    pool_new = mix3[:, seq_len - POOL_HIST:, MIX_U:MIX_U + MIX]
    k_new = mix3[:, seq_len - keep:, MIX_KC:MIX_KC + MIX].reshape(n_seq, keep, ATT_HEADS, ATT_DH)
    v_new = mix3[:, seq_len - keep:, MIX_VC:MIX_VC + MIX].reshape(n_seq, keep, ATT_HEADS, ATT_DH)
    gla_new = (s_fin_t.reshape(n_seq, 2, GLA_DV, 2, GLA_DK).transpose(0, 1, 3, 4, 2)
               .reshape(n_seq, GLA_HEADS, GLA_DK, GLA_DV))
    return x_new, (pool_new, gla_new, k_new, v_new)


def kernel(x_prompt, x_sample, cache_pool, state_gla, cache_k, cache_v, attn_norm_g, w_in, w_gate2, b_gate,
           gla_norm_g, pool_map, pool_scale, rel_bias, w_branch, w_out, ffn_norm_g, w_ffn_in, w_ffn_out,
           final_norm_g):
    batch, seq, _ = x_prompt.shape
    dec_batch, dec_seq, _ = x_sample.shape
    depth = w_in.shape[0]
    hp = x_prompt.reshape(batch * seq, D_MODEL)
    hs = x_sample.reshape(dec_batch * dec_seq, D_MODEL)
    outs_p, outs_s = [], []
    for l in range(depth):
        lw = _layer_weights(l, attn_norm_g, w_in, w_gate2, b_gate, gla_norm_g, pool_map, pool_scale, rel_bias,
                            w_branch, w_out, ffn_norm_g, w_ffn_in, w_ffn_out, dec_seq)
        fg = final_norm_g[None] if l == depth - 1 else None
        hp, sp = _layer(hp, batch, seq, None, lw, 0, fg)
        hs, ss = _layer(hs, dec_batch, dec_seq, (cache_pool[l], state_gla[l], cache_k[l], cache_v[l]), lw,
                        PAST_LEN, fg)
        outs_p.append(sp)
        outs_s.append(ss)
    stack = lambda outs, j: jnp.stack([o[j] for o in outs])
    return (hp.reshape(batch, seq, D_MODEL), hs.reshape(dec_batch, dec_seq, D_MODEL),
            stack(outs_p, 0), stack(outs_s, 0), stack(outs_p, 1), stack(outs_s, 1),
            stack(outs_p, 2), stack(outs_s, 2), stack(outs_p, 3), stack(outs_s, 3))
```

```python
import functools

import jax
import jax.numpy as jnp
from jax import lax
from jax.experimental import pallas as pl
from jax.experimental.pallas import tpu as pltpu

F32 = jnp.float32
BF16 = jnp.bfloat16

D_MODEL = 1024
PAST_LEN = 2048
CHUNK = 64
MIX = 512
POOL_WINDOWS = (2, 4, 8, 16)
POOL_GW = 128
POOL_HIST = 15
HIST_ROWS = 16
GLA_HEADS = 4
GLA_DK = 64
GLA_DV = 128
GLA_QK_W = 256
GLA_RANK = 16
GLA_TAU = 16.0
ATT_HEADS = 8
ATT_DH = 64
ATT_WIN = 512
MAX_REL = 128
N_BRANCH = 3
D_FF = 2816
EPS = 1e-6
MASKED = -1e30
LANES = 128

MIX_U, MIX_VB, MIX_RB, MIX_QC, MIX_KC, MIX_VC = 0, 512, 1024, 1536, 2048, 2560
MIX_W = 3072
GLA_Q, GLA_K, GLA_Z = 0, 256, 512
GLA_W = 640
GATE_W = N_BRANCH * D_MODEL
YS_A, YS_B, YS_C = 0, 512, 1024

ROW_TILE = 256
SEQ_TILE = 512
VMEM_LIMIT = 56 * 1024 * 1024

NT_DIMS = (((1,), (1,)), ((), ()))
TN_DIMS = (((0,), (0,)), ((), ()))


def _sigmoid(x):
    return 1.0 / (1.0 + jnp.exp(-x))


def _rms(x, g):
    return x * lax.rsqrt(jnp.mean(x * x, axis=-1, keepdims=True) + EPS) * g


def _const_spec(shape):
    nd = len(shape)
    return pl.BlockSpec(shape, lambda *_: (0,) * nd, pipeline_mode=pl.Buffered(1))


def _inproj_kernel(x_ref, g_ref, wg_ref, wm_ref, wl_ref, og_ref, om_ref, ol_ref):
    xb = _rms(x_ref[...], g_ref[...]).astype(BF16)
    og_ref[...] = jnp.dot(xb, wg_ref[...], preferred_element_type=F32)
    om_ref[...] = jnp.dot(xb, wm_ref[...], preferred_element_type=F32)
    ol_ref[...] = jnp.dot(xb, wl_ref[...], preferred_element_type=F32)


def _inproj(x, g, w_gates, w_mix, w_gla):
    rows = x.shape[0]
    tm = min(ROW_TILE, rows)
    row_spec = lambda w: pl.BlockSpec((tm, w), lambda i: (i, 0))
    return pl.pallas_call(
        _inproj_kernel,
        grid=(rows // tm,),
        in_specs=[row_spec(D_MODEL), _const_spec((1, D_MODEL)), _const_spec(w_gates.shape),
                  _const_spec(w_mix.shape), _const_spec(w_gla.shape)],
        out_specs=[row_spec(GATE_W), row_spec(MIX_W), row_spec(GLA_W)],
        out_shape=[jax.ShapeDtypeStruct((rows, GATE_W), F32), jax.ShapeDtypeStruct((rows, MIX_W), F32),
                   jax.ShapeDtypeStruct((rows, GLA_W), F32)],
        compiler_params=pltpu.CompilerParams(dimension_semantics=("arbitrary",), vmem_limit_bytes=VMEM_LIMIT),
        name="in_proj",
    )(x, g, w_gates, w_mix, w_gla)


def _mixer_kernel(tt, lg, lq, grp, has_state, pos0, *refs):
    refs = list(refs)
    mix_ref, gla_ref = refs[:2]
    k = 2
    if has_state:
        hist_ref, s0_ref, kc_ref, vc_ref = refs[k:k + 4]
        k += 4
    wg2_ref, bg_ref, gng_ref, pmap_ref, pscale_ref, bias_ref = refs[k:k + 6]
    k += 6
    ys_ref, sfin_ref = refs[k:k + 2]
    carry_ref, st_ref, kbuf_ref, vbuf_ref = refs[k + 2:]

    i = pl.program_id(1)
    lane = lax.broadcasted_iota(jnp.int32, (1, LANES), 1)
    head_mask = (lane < ATT_DH, lane >= ATT_DH)

    @pl.when(i == 0)
    def _init():
        if has_state:
            carry_ref[...] = hist_ref[0]
            st_ref[...] = s0_ref[0]
            kbuf_ref[0:ATT_WIN, :] = kc_ref[0].astype(BF16)
            vbuf_ref[0:ATT_WIN, :] = vc_ref[0].astype(BF16)
        else:
            carry_ref[...] = jnp.zeros_like(carry_ref)
            st_ref[...] = jnp.zeros_like(st_ref)
            kbuf_ref[0:ATT_WIN, :] = jnp.zeros((ATT_WIN, MIX), BF16)
            vbuf_ref[0:ATT_WIN, :] = jnp.zeros((ATT_WIN, MIX), BF16)

    u = mix_ref[:, MIX_U:MIX_U + MIX]
    ext = jnp.concatenate([carry_ref[...], u], axis=0)
    carry_ref[...] = u[tt - HIST_ROWS:, :]
    row = lax.broadcasted_iota(jnp.int32, (tt, POOL_GW), 0)
    n_seen = (pos0 + i * tt + row + 1).astype(F32)
    for g, w in enumerate(POOL_WINDOWS):
        cols = slice(g * POOL_GW, (g + 1) * POOL_GW)
        s = ext[:, cols]
        shift = 1
        while shift < w:
            s = s + pltpu.roll(s, shift, 0)
            shift *= 2
        p = s[HIST_ROWS:, :] / jnp.minimum(float(w), n_seen) - u[:, cols]
        y = jnp.dot(p.astype(BF16), pmap_ref[g], preferred_element_type=F32) * pscale_ref[:, cols]
        ys_ref[:, YS_A + g * POOL_GW:YS_A + (g + 1) * POOL_GW] = y.astype(BF16)

    z = gla_ref[:, GLA_Z:GLA_Z + LANES].astype(BF16)
    pre = jnp.dot(z, wg2_ref[...], preferred_element_type=F32) + bg_ref[...]
    log_a = -(jnp.maximum(-pre, 0.0) + jnp.log1p(jnp.exp(-jnp.abs(pre)))) / GLA_TAU
    row_in_blk = lax.broadcasted_iota(jnp.int32, (tt, GLA_QK_W), 0) & (lg - 1)
    b = log_a
    shift = 1
    while shift < lg:
        b = b + jnp.where(row_in_blk >= shift, pltpu.roll(b, shift, 0), 0.0)
        shift *= 2

    ri = lax.broadcasted_iota(jnp.int32, (2 * lg, lg), 0) & (lg - 1)
    ci = lax.broadcasted_iota(jnp.int32, (2 * lg, lg), 1)
    causal = ci <= ri

    def stack_heads(x):
        return jnp.concatenate([jnp.where(head_mask[0], x, 0.0), jnp.where(head_mask[1], x, 0.0)],
                               axis=0).astype(BF16)

    for p in range(GLA_HEADS // 2):
        lanes = slice(p * LANES, (p + 1) * LANES)
        bq = b[:, lanes]
        eb = jnp.exp(bq)
        enb = jnp.exp(-bq)
        qs = gla_ref[:, GLA_Q + p * LANES:GLA_Q + (p + 1) * LANES] * (GLA_DK ** -0.5) * eb
        kt = gla_ref[:, GLA_K + p * LANES:GLA_K + (p + 1) * LANES] * enb
        st = st_ref[p]
        for c in range(tt // lg):
            rows = slice(c * lg, (c + 1) * lg)
            e_last = eb[(c + 1) * lg - 1:(c + 1) * lg, :]
            ktb = kt[rows].astype(BF16)
            kdb = (kt[rows] * e_last).astype(BF16)
            qst = stack_heads(qs[rows])
            att = lax.dot_general(qst, ktb, NT_DIMS, preferred_element_type=F32)
            att = jnp.where(causal, att, 0.0).astype(BF16)
            inter = lax.dot_general(qst, st.astype(BF16), NT_DIMS, preferred_element_type=F32)
            upd = []
            for hl in range(2):
                h = 2 * p + hl
                half = slice(hl * lg, (hl + 1) * lg)
                vh = mix_ref[rows, MIX_VB + h * GLA_DV:MIX_VB + (h + 1) * GLA_DV].astype(BF16)
                o = jnp.dot(att[half], vh, preferred_element_type=F32) + inter[half]
                o = _rms(o, gng_ref[...])
                r = mix_ref[rows, MIX_RB + h * GLA_DV:MIX_RB + (h + 1) * GLA_DV]
                ys_ref[rows, YS_B + h * GLA_DV:YS_B + (h + 1) * GLA_DV] = (o * (r * _sigmoid(r))).astype(BF16)
                upd.append(lax.dot_general(vh, kdb, TN_DIMS, preferred_element_type=F32))
            st = e_last * st + jnp.where(head_mask[0], upd[0], upd[1])
        st_ref[p] = st

    kbuf_ref[ATT_WIN:ATT_WIN + tt, :] = mix_ref[:, MIX_KC:MIX_KC + MIX].astype(BF16)
    vbuf_ref[ATT_WIN:ATT_WIN + tt, :] = mix_ref[:, MIX_VC:MIX_VC + MIX].astype(BF16)
    gq = grp * lq
    win = ATT_WIN + gq
    kcol = lax.broadcasted_iota(jnp.int32, (1, win), 1)
    first_tile_neg = jnp.where(i == 0, MASKED, 0.0).astype(F32)
    for gi in range(tt // gq):
        qrows = slice(gi * gq, (gi + 1) * gq)
        krows = slice(gi * gq, gi * gq + win)
        if not has_state:
            invalid = jnp.where(kcol < ATT_WIN - gi * gq, first_tile_neg, 0.0)
        for p in range(ATT_HEADS // 2):
            cols = slice(p * LANES, (p + 1) * LANES)
            q2 = mix_ref[qrows, MIX_QC + p * LANES:MIX_QC + (p + 1) * LANES] * (ATT_DH ** -0.5)
            s = lax.dot_general(stack_heads(q2), kbuf_ref[krows, cols], NT_DIMS, preferred_element_type=F32)
            s = s + bias_ref[p]
            if not has_state:
                s = s + invalid
            e = jnp.exp(s - jnp.max(s, axis=-1, keepdims=True))
            den = jnp.sum(e, axis=-1, keepdims=True)
            o = jnp.dot(e.astype(BF16), vbuf_ref[krows, cols], preferred_element_type=F32) / den
            ys_ref[qrows, YS_C + p * LANES:YS_C + (p + 1) * LANES] = (
                jnp.where(head_mask[0], o[:gq], o[gq:]).astype(BF16))

    if not has_state:
        kbuf_ref[0:ATT_WIN, :] = kbuf_ref[tt:tt + ATT_WIN, :]
        vbuf_ref[0:ATT_WIN, :] = vbuf_ref[tt:tt + ATT_WIN, :]

    @pl.when(i == pl.num_programs(1) - 1)
    def _fin():
        sfin_ref[0] = st_ref[...]


def _mixers(mix, gla, state, lw, n_seq, seq_len, pos0):
    has_state = state is not None
    tt = min(SEQ_TILE, seq_len)
    lg = min(CHUNK, seq_len)
    lq = min(CHUNK, seq_len)
    grp = _att_group(seq_len)
    nt = seq_len // tt
    assert has_state or tt == ATT_WIN
    kernel = functools.partial(_mixer_kernel, tt, lg, lq, grp, has_state, pos0)

    tile = lambda w: pl.BlockSpec((tt, w), lambda s, i: (s * nt + i, 0))
    in_specs = [tile(MIX_W), tile(GLA_W)]
    args = [mix, gla]
    if has_state:
        hist, s0t, kc, vc = state
        in_specs += [pl.BlockSpec((1, HIST_ROWS, MIX), lambda s, i: (s, 0, 0)),
                     pl.BlockSpec((1, 2, LANES, LANES), lambda s, i: (s, 0, 0, 0)),
                     pl.BlockSpec((1, ATT_WIN, MIX), lambda s, i: (s, 0, 0)),
                     pl.BlockSpec((1, ATT_WIN, MIX), lambda s, i: (s, 0, 0))]
        args += [hist, s0t, kc, vc]
    consts = [lw["w_gate2"], lw["b_gate"], lw["gla_norm_g"], lw["pool_map"], lw["pool_scale"],
              lw["bias_s"] if has_state else lw["bias_p"]]
    in_specs += [_const_spec(c.shape) for c in consts]
    args += consts

    return pl.pallas_call(
        kernel,
        grid=(n_seq, nt),
        in_specs=in_specs,
        out_specs=[pl.BlockSpec((tt, 3 * MIX), lambda s, i: (s * nt + i, 0)),
                   pl.BlockSpec((1, 2, LANES, LANES), lambda s, i: (s, 0, 0, 0))],
        out_shape=[jax.ShapeDtypeStruct((n_seq * seq_len, 3 * MIX), BF16),
                   jax.ShapeDtypeStruct((n_seq, 2, LANES, LANES), F32)],
        scratch_shapes=[pltpu.VMEM((HIST_ROWS, MIX), F32),
                        pltpu.VMEM((2, LANES, LANES), F32),
                        pltpu.VMEM((ATT_WIN + tt, MIX), BF16),
                        pltpu.VMEM((ATT_WIN + tt, MIX), BF16)],
        compiler_params=pltpu.CompilerParams(dimension_semantics=("arbitrary", "arbitrary"),
                                             vmem_limit_bytes=VMEM_LIMIT),
        name="mixers_sample" if has_state else "mixers_prompt",
    )(*args)


def _merge_ffn_kernel(final, x_ref, ys_ref, hg_ref, wbr_ref, wout_ref, g2_ref, wa_ref, wgt_ref, wo_ref, fg_ref,
                      out_ref):
    merged = None
    for g in range(N_BRANCH):
        branch = jnp.dot(ys_ref[:, g * MIX:(g + 1) * MIX], wbr_ref[g], preferred_element_type=F32)
        term = _sigmoid(hg_ref[:, g * D_MODEL:(g + 1) * D_MODEL]) * branch
        merged = term if merged is None else merged + term
    x1 = x_ref[...] + jnp.dot(merged.astype(BF16), wout_ref[...], preferred_element_type=F32)
    xn = _rms(x1, g2_ref[...]).astype(BF16)
    a = jnp.dot(xn, wa_ref[...], preferred_element_type=F32)
    gt = jnp.dot(xn, wgt_ref[...], preferred_element_type=F32)
    act = (a * _sigmoid(a) * gt).astype(BF16)
    x2 = x1 + jnp.dot(act, wo_ref[...], preferred_element_type=F32)
    if final:
        x2 = _rms(x2, fg_ref[...])
    out_ref[...] = x2


def _merge_ffn(x, ys, gates, lw, final_g):
    rows = x.shape[0]
    tm = min(ROW_TILE, rows)
    final = final_g is not None
    fg = final_g if final else lw["ffn_norm_g"]
    row_spec = lambda w: pl.BlockSpec((tm, w), lambda i: (i, 0))
    consts = [lw["w_branch"], lw["w_out"], lw["ffn_norm_g"], lw["w_ffn_a"], lw["w_ffn_g"], lw["w_ffn_out"], fg]
    return pl.pallas_call(
        functools.partial(_merge_ffn_kernel, final),
        grid=(rows // tm,),
        in_specs=[row_spec(D_MODEL), row_spec(3 * MIX), row_spec(GATE_W)] + [_const_spec(c.shape) for c in consts],
        out_specs=row_spec(D_MODEL),
        out_shape=jax.ShapeDtypeStruct((rows, D_MODEL), F32),
        compiler_params=pltpu.CompilerParams(dimension_semantics=("arbitrary",), vmem_limit_bytes=VMEM_LIMIT),
        name="merge_ffn",
    )(x, ys, gates, *consts)


def _att_group(seq_len):
    return 2 if seq_len >= 2 * CHUNK else 1


def _band_bias(rel_bias, lq, grp):
    assert lq - 1 <= MAX_REL
    rtab = rel_bias[:, ::-1].astype(F32)
    n_far = ATT_WIN + lq - MAX_REL
    ext = jnp.concatenate([jnp.broadcast_to(rtab[:, :1], (ATT_HEADS, n_far)), rtab[:, 1:MAX_REL + lq]], axis=1)
    wk = ATT_WIN + lq
    chunk = jnp.stack([ext[:, lq - 1 - q:lq - 1 - q + wk] for q in range(lq)], axis=1)
    blocks = [jnp.pad(chunk, ((0, 0), (0, 0), (j * lq, (grp - 1 - j) * lq)), constant_values=MASKED)
              for j in range(grp)]
    per_head = jnp.concatenate(blocks, axis=1)
    return per_head.reshape(ATT_HEADS // 2, 2 * grp * lq, ATT_WIN + grp * lq)


def _layer_weights(l, attn_norm_g, w_in, w_gate2, b_gate, gla_norm_g, pool_map, pool_scale, rel_bias, w_branch,
                   w_out, ffn_norm_g, w_ffn_in, w_ffn_out, prompt_len, sample_len):
    o = [0, 512, 768, 1024, 1536, 2048, 2064, 2576, 3088, 3600, 6672]
    w = w_in[l]
    col = lambda a, b: w[:, o[a]:o[b]]
    w_mix = jnp.concatenate([col(0, 1), col(3, 4), col(4, 5), col(6, 7), col(7, 8), col(8, 9)], axis=1)
    w_gla = jnp.concatenate([col(1, 2), col(2, 3), col(5, 6),
                             jnp.zeros((D_MODEL, LANES - GLA_RANK), w.dtype)], axis=1)
    wg2 = jnp.concatenate([w_gate2[l], jnp.zeros((LANES - GLA_RANK, GLA_QK_W), w_gate2.dtype)], axis=0)
    return dict(
        attn_norm_g=attn_norm_g[l][None], w_gates=col(9, 10).astype(BF16), w_mix=w_mix.astype(BF16),
        w_gla=w_gla.astype(BF16), w_gate2=wg2.astype(BF16), b_gate=b_gate[l][None],
        gla_norm_g=gla_norm_g[l][None], pool_map=pool_map[l].astype(BF16), pool_scale=pool_scale[l][None],
        bias_p=_band_bias(rel_bias[l], CHUNK, _att_group(prompt_len)),
        bias_s=_band_bias(rel_bias[l], min(CHUNK, sample_len), _att_group(sample_len)),
        w_branch=w_branch[l].astype(BF16), w_out=w_out[l].astype(BF16), ffn_norm_g=ffn_norm_g[l][None],
        w_ffn_a=w_ffn_in[l][:, :D_FF].astype(BF16), w_ffn_g=w_ffn_in[l][:, D_FF:].astype(BF16),
        w_ffn_out=w_ffn_out[l].astype(BF16))


def _layer(x, n_seq, seq_len, state, lw, pos0, final_g):
    gates, mix, gla = _inproj(x, lw["attn_norm_g"], lw["w_gates"], lw["w_mix"], lw["w_gla"])
    if state is not None:
        hist, s0, kc, vc = state
        hist = jnp.pad(hist, ((0, 0), (HIST_ROWS - POOL_HIST, 0), (0, 0)))
        s0t = s0.reshape(n_seq, 2, 2, GLA_DK, GLA_DV).transpose(0, 1, 4, 2, 3).reshape(n_seq, 2, LANES, LANES)
        state = (hist, s0t, kc.reshape(n_seq, ATT_WIN, MIX), vc.reshape(n_seq, ATT_WIN, MIX))
    ys, s_fin_t = _mixers(mix, gla, state, lw, n_seq, seq_len, pos0)
    x_new = _merge_ffn(x, ys, gates, lw, final_g)

    mix3 = mix.reshape(n_seq, seq_len, MIX_W)
    keep = min(ATT_WIN, seq_len)
    pool_new = mix3[:, seq_len - POOL_HIST:, MIX_U:MIX_U + MIX]
    k_new = mix3[:, seq_len - keep:, MIX_KC:MIX_KC + MIX].reshape(n_seq, keep, ATT_HEADS, ATT_DH)
    v_new = mix3[:, seq_len - keep:, MIX_VC:MIX_VC + MIX].reshape(n_seq, keep, ATT_HEADS, ATT_DH)
    gla_new = (s_fin_t.reshape(n_seq, 2, GLA_DV, 2, GLA_DK).transpose(0, 1, 3, 4, 2)
               .reshape(n_seq, GLA_HEADS, GLA_DK, GLA_DV))
    return x_new, (pool_new, gla_new, k_new, v_new)


def kernel(x_prompt, x_sample, cache_pool, state_gla, cache_k, cache_v, attn_norm_g, w_in, w_gate2, b_gate,
           gla_norm_g, pool_map, pool_scale, rel_bias, w_branch, w_out, ffn_norm_g, w_ffn_in, w_ffn_out,
           final_norm_g):
    batch, seq, _ = x_prompt.shape
    dec_batch, dec_seq, _ = x_sample.shape
    depth = w_in.shape[0]
    hp = x_prompt.reshape(batch * seq, D_MODEL)
    hs = x_sample.reshape(dec_batch * dec_seq, D_MODEL)
    outs_p, outs_s = [], []
    for l in range(depth):
        lw = _layer_weights(l, attn_norm_g, w_in, w_gate2, b_gate, gla_norm_g, pool_map, pool_scale, rel_bias,
                            w_branch, w_out, ffn_norm_g, w_ffn_in, w_ffn_out, seq, dec_seq)
        fg = final_norm_g[None] if l == depth - 1 else None
        hp, sp = _layer(hp, batch, seq, None, lw, 0, fg)
        hs, ss = _layer(hs, dec_batch, dec_seq, (cache_pool[l], state_gla[l], cache_k[l], cache_v[l]), lw,
                        PAST_LEN, fg)
        outs_p.append(sp)
        outs_s.append(ss)
    stack = lambda outs, j: jnp.stack([o[j] for o in outs])
    return (hp.reshape(batch, seq, D_MODEL), hs.reshape(dec_batch, dec_seq, D_MODEL),
            stack(outs_p, 0), stack(outs_s, 0), stack(outs_p, 1), stack(outs_s, 1),
            stack(outs_p, 2), stack(outs_s, 2), stack(outs_p, 3), stack(outs_s, 3))
```

```python
import functools

import jax
import jax.numpy as jnp
from jax import lax
from jax.experimental import pallas as pl
from jax.experimental.pallas import tpu as pltpu

F32 = jnp.float32
BF16 = jnp.bfloat16

D_MODEL = 1024
PAST_LEN = 2048
CHUNK = 64
MIX = 512
POOL_WINDOWS = (2, 4, 8, 16)
POOL_GW = 128
POOL_HIST = 15
HIST_ROWS = 16
GLA_HEADS = 4
GLA_DK = 64
GLA_DV = 128
GLA_QK_W = 256
GLA_RANK = 16
GLA_TAU = 16.0
ATT_HEADS = 8
ATT_DH = 64
ATT_WIN = 512
MAX_REL = 128
N_BRANCH = 3
D_FF = 2816
EPS = 1e-6
MASKED = -1e30
LANES = 128

MIX_U, MIX_VB, MIX_RB, MIX_QC, MIX_KC, MIX_VC = 0, 512, 1024, 1536, 2048, 2560
MIX_W = 3072
GLA_Q, GLA_K, GLA_Z = 0, 256, 512
GLA_W = 640
GATE_W = N_BRANCH * D_MODEL
YS_A, YS_B, YS_C = 0, 512, 1024

ROW_TILE = 256
SEQ_TILE = 512
VMEM_LIMIT = 56 * 1024 * 1024

NT_DIMS = (((1,), (1,)), ((), ()))
TN_DIMS = (((0,), (0,)), ((), ()))


def _sigmoid(x):
    return 1.0 / (1.0 + jnp.exp(-x))


def _rms(x, g):
    return x * lax.rsqrt(jnp.mean(x * x, axis=-1, keepdims=True) + EPS) * g


def _layer_spec(arr, l):
    tail = (0,) * (arr.ndim - 1)
    return pl.BlockSpec((None,) + arr.shape[1:], lambda *_: (l,) + tail, pipeline_mode=pl.Buffered(1))


def _inproj_kernel(x_ref, g_ref, wg_ref, wm_ref, wl_ref, og_ref, om_ref, ol_ref):
    xb = _rms(x_ref[...], g_ref[...]).astype(BF16)
    og_ref[...] = jnp.dot(xb, wg_ref[...], preferred_element_type=F32)
    om_ref[...] = jnp.dot(xb, wm_ref[...], preferred_element_type=F32)
    ol_ref[...] = jnp.dot(xb, wl_ref[...], preferred_element_type=F32)


def _inproj(x, pw, l):
    rows = x.shape[0]
    tm = min(ROW_TILE, rows)
    row_spec = lambda w: pl.BlockSpec((tm, w), lambda i: (i, 0))
    consts = [pw["attn_norm_g"], pw["w_gates"], pw["w_mix"], pw["w_gla"]]
    return pl.pallas_call(
        _inproj_kernel,
        grid=(rows // tm,),
        in_specs=[row_spec(D_MODEL)] + [_layer_spec(c, l) for c in consts],
        out_specs=[row_spec(GATE_W), row_spec(MIX_W), row_spec(GLA_W)],
        out_shape=[jax.ShapeDtypeStruct((rows, GATE_W), F32), jax.ShapeDtypeStruct((rows, MIX_W), F32),
                   jax.ShapeDtypeStruct((rows, GLA_W), F32)],
        compiler_params=pltpu.CompilerParams(dimension_semantics=("arbitrary",), vmem_limit_bytes=VMEM_LIMIT),
        name="in_proj",
    )(x, *consts)


def _mixer_kernel(tt, lg, lq, grp, has_state, pos0, *refs):
    refs = list(refs)
    mix_ref, gla_ref = refs[:2]
    k = 2
    if has_state:
        hist_ref, s0_ref, kc_ref, vc_ref = refs[k:k + 4]
        k += 4
    wg2_ref, bg_ref, gng_ref, pmap_ref, pscale_ref, bias_ref = refs[k:k + 6]
    k += 6
    ys_ref, sfin_ref, pool_ref, knew_ref, vnew_ref = refs[k:k + 5]
    carry_ref, st_ref, kbuf_ref, vbuf_ref = refs[k + 5:]

    i = pl.program_id(1)
    lane = lax.broadcasted_iota(jnp.int32, (1, LANES), 1)
    head_mask = (lane < ATT_DH, lane >= ATT_DH)

    @pl.when(i == 0)
    def _init():
        if has_state:
            carry_ref[...] = hist_ref[0]
            st_ref[...] = s0_ref[0]
            kbuf_ref[0:ATT_WIN, :] = kc_ref[0].astype(BF16)
            vbuf_ref[0:ATT_WIN, :] = vc_ref[0].astype(BF16)
        else:
            carry_ref[...] = jnp.zeros_like(carry_ref)
            st_ref[...] = jnp.zeros_like(st_ref)
            kbuf_ref[0:ATT_WIN, :] = jnp.zeros((ATT_WIN, MIX), BF16)
            vbuf_ref[0:ATT_WIN, :] = jnp.zeros((ATT_WIN, MIX), BF16)

    u = mix_ref[:, MIX_U:MIX_U + MIX]
    ext = jnp.concatenate([carry_ref[...], u], axis=0)
    carry_ref[...] = u[tt - HIST_ROWS:, :]
    row = lax.broadcasted_iota(jnp.int32, (tt, POOL_GW), 0)
    n_seen = (pos0 + i * tt + row + 1).astype(F32)
    for g, w in enumerate(POOL_WINDOWS):
        cols = slice(g * POOL_GW, (g + 1) * POOL_GW)
        s = ext[:, cols]
        shift = 1
        while shift < w:
            s = s + pltpu.roll(s, shift, 0)
            shift *= 2
        p = s[HIST_ROWS:, :] / jnp.minimum(float(w), n_seen) - u[:, cols]
        y = jnp.dot(p.astype(BF16), pmap_ref[g], preferred_element_type=F32) * pscale_ref[:, cols]
        ys_ref[:, YS_A + g * POOL_GW:YS_A + (g + 1) * POOL_GW] = y.astype(BF16)

    z = gla_ref[:, GLA_Z:GLA_Z + LANES].astype(BF16)
    pre = jnp.dot(z, wg2_ref[...], preferred_element_type=F32) + bg_ref[...]
    log_a = -(jnp.maximum(-pre, 0.0) + jnp.log1p(jnp.exp(-jnp.abs(pre)))) / GLA_TAU
    row_in_blk = lax.broadcasted_iota(jnp.int32, (tt, GLA_QK_W), 0) & (lg - 1)
    b = log_a
    shift = 1
    while shift < lg:
        b = b + jnp.where(row_in_blk >= shift, pltpu.roll(b, shift, 0), 0.0)
        shift *= 2

    ri = lax.broadcasted_iota(jnp.int32, (2 * lg, lg), 0) & (lg - 1)
    ci = lax.broadcasted_iota(jnp.int32, (2 * lg, lg), 1)
    causal = ci <= ri

    def stack_heads(x):
        return jnp.concatenate([jnp.where(head_mask[0], x, 0.0), jnp.where(head_mask[1], x, 0.0)],
                               axis=0).astype(BF16)

    for p in range(GLA_HEADS // 2):
        lanes = slice(p * LANES, (p + 1) * LANES)
        bq = b[:, lanes]
        eb = jnp.exp(bq)
        enb = jnp.exp(-bq)
        qs = gla_ref[:, GLA_Q + p * LANES:GLA_Q + (p + 1) * LANES] * (GLA_DK ** -0.5) * eb
        kt = gla_ref[:, GLA_K + p * LANES:GLA_K + (p + 1) * LANES] * enb
        st = st_ref[p]
        for c in range(tt // lg):
            rows = slice(c * lg, (c + 1) * lg)
            e_last = eb[(c + 1) * lg - 1:(c + 1) * lg, :]
            ktb = kt[rows].astype(BF16)
            kdb = (kt[rows] * e_last).astype(BF16)
            qst = stack_heads(qs[rows])
            att = lax.dot_general(qst, ktb, NT_DIMS, preferred_element_type=F32)
            att = jnp.where(causal, att, 0.0).astype(BF16)
            inter = lax.dot_general(qst, st.astype(BF16), NT_DIMS, preferred_element_type=F32)
            upd = []
            for hl in range(2):
                h = 2 * p + hl
                half = slice(hl * lg, (hl + 1) * lg)
                vh = mix_ref[rows, MIX_VB + h * GLA_DV:MIX_VB + (h + 1) * GLA_DV].astype(BF16)
                o = jnp.dot(att[half], vh, preferred_element_type=F32) + inter[half]
                o = _rms(o, gng_ref[...])
                r = mix_ref[rows, MIX_RB + h * GLA_DV:MIX_RB + (h + 1) * GLA_DV]
                ys_ref[rows, YS_B + h * GLA_DV:YS_B + (h + 1) * GLA_DV] = (o * (r * _sigmoid(r))).astype(BF16)
                upd.append(lax.dot_general(vh, kdb, TN_DIMS, preferred_element_type=F32))
            st = e_last * st + jnp.where(head_mask[0], upd[0], upd[1])
        st_ref[p] = st

    kbuf_ref[ATT_WIN:ATT_WIN + tt, :] = mix_ref[:, MIX_KC:MIX_KC + MIX].astype(BF16)
    vbuf_ref[ATT_WIN:ATT_WIN + tt, :] = mix_ref[:, MIX_VC:MIX_VC + MIX].astype(BF16)
    gq = grp * lq
    win = ATT_WIN + gq
    kcol = lax.broadcasted_iota(jnp.int32, (1, win), 1)
    first_tile_neg = jnp.where(i == 0, MASKED, 0.0).astype(F32)
    for gi in range(tt // gq):
        qrows = slice(gi * gq, (gi + 1) * gq)
        krows = slice(gi * gq, gi * gq + win)
        if not has_state:
            invalid = jnp.where(kcol < ATT_WIN - gi * gq, first_tile_neg, 0.0)
        for p in range(ATT_HEADS // 2):
            cols = slice(p * LANES, (p + 1) * LANES)
            q2 = mix_ref[qrows, MIX_QC + p * LANES:MIX_QC + (p + 1) * LANES] * (ATT_DH ** -0.5)
            s = lax.dot_general(stack_heads(q2), kbuf_ref[krows, cols], NT_DIMS, preferred_element_type=F32)
            s = s + bias_ref[p]
            if not has_state:
                s = s + invalid
            e = jnp.exp(s - jnp.max(s, axis=-1, keepdims=True))
            den = jnp.sum(e, axis=-1, keepdims=True)
            o = jnp.dot(e.astype(BF16), vbuf_ref[krows, cols], preferred_element_type=F32) / den
            ys_ref[qrows, YS_C + p * LANES:YS_C + (p + 1) * LANES] = (
                jnp.where(head_mask[0], o[:gq], o[gq:]).astype(BF16))

    if not has_state:
        kbuf_ref[0:ATT_WIN, :] = kbuf_ref[tt:tt + ATT_WIN, :]
        vbuf_ref[0:ATT_WIN, :] = vbuf_ref[tt:tt + ATT_WIN, :]

    @pl.when(i == pl.num_programs(1) - 1)
    def _fin():
        keep = min(ATT_WIN, tt)
        sfin_ref[0] = st_ref[...]
        pool_ref[0] = mix_ref[tt - HIST_ROWS:, MIX_U:MIX_U + MIX]
        knew_ref[0] = mix_ref[tt - keep:, MIX_KC:MIX_KC + MIX]
        vnew_ref[0] = mix_ref[tt - keep:, MIX_VC:MIX_VC + MIX]


def _mixers(mix, gla, state, pw, l, n_seq, seq_len, pos0):
    has_state = state is not None
    tt = min(SEQ_TILE, seq_len)
    lg = min(CHUNK, seq_len)
    lq = min(CHUNK, seq_len)
    grp = _att_group(seq_len)
    nt = seq_len // tt
    assert has_state or tt == ATT_WIN
    kernel = functools.partial(_mixer_kernel, tt, lg, lq, grp, has_state, pos0)

    tile = lambda w: pl.BlockSpec((tt, w), lambda s, i: (s * nt + i, 0))
    in_specs = [tile(MIX_W), tile(GLA_W)]
    args = [mix, gla]
    if has_state:
        hist, s0t, kc, vc = state
        in_specs += [pl.BlockSpec((1, HIST_ROWS, MIX), lambda s, i: (s, 0, 0)),
                     pl.BlockSpec((1, 2, LANES, LANES), lambda s, i: (s, 0, 0, 0)),
                     pl.BlockSpec((1, ATT_WIN, MIX), lambda s, i: (s, 0, 0)),
                     pl.BlockSpec((1, ATT_WIN, MIX), lambda s, i: (s, 0, 0))]
        args += [hist, s0t, kc, vc]
    consts = [pw["w_gate2"], pw["b_gate"], pw["gla_norm_g"], pw["pool_map"], pw["pool_scale"],
              pw["bias_s"] if has_state else pw["bias_p"]]
    in_specs += [_layer_spec(c, l) for c in consts]
    args += consts
    keep = min(ATT_WIN, seq_len)
    per_seq = lambda r: pl.BlockSpec((1, r, MIX), lambda s, i: (s, 0, 0))

    return pl.pallas_call(
        kernel,
        grid=(n_seq, nt),
        in_specs=in_specs,
        out_specs=[pl.BlockSpec((tt, 3 * MIX), lambda s, i: (s * nt + i, 0)),
                   pl.BlockSpec((1, 2, LANES, LANES), lambda s, i: (s, 0, 0, 0)),
                   per_seq(HIST_ROWS), per_seq(keep), per_seq(keep)],
        out_shape=[jax.ShapeDtypeStruct((n_seq * seq_len, 3 * MIX), BF16),
                   jax.ShapeDtypeStruct((n_seq, 2, LANES, LANES), F32),
                   jax.ShapeDtypeStruct((n_seq, HIST_ROWS, MIX), F32),
                   jax.ShapeDtypeStruct((n_seq, keep, MIX), F32),
                   jax.ShapeDtypeStruct((n_seq, keep, MIX), F32)],
        scratch_shapes=[pltpu.VMEM((HIST_ROWS, MIX), F32),
                        pltpu.VMEM((2, LANES, LANES), F32),
                        pltpu.VMEM((ATT_WIN + tt, MIX), BF16),
                        pltpu.VMEM((ATT_WIN + tt, MIX), BF16)],
        compiler_params=pltpu.CompilerParams(dimension_semantics=("arbitrary", "arbitrary"),
                                             vmem_limit_bytes=VMEM_LIMIT),
        name="mixers_sample" if has_state else "mixers_prompt",
    )(*args)


def _merge_ffn_kernel(final, x_ref, ys_ref, hg_ref, wbr_ref, wout_ref, g2_ref, wa_ref, wgt_ref, wo_ref, fg_ref,
                      out_ref):
    merged = None
    for g in range(N_BRANCH):
        branch = jnp.dot(ys_ref[:, g * MIX:(g + 1) * MIX], wbr_ref[g], preferred_element_type=F32)
        term = _sigmoid(hg_ref[:, g * D_MODEL:(g + 1) * D_MODEL]) * branch
        merged = term if merged is None else merged + term
    x1 = x_ref[...] + jnp.dot(merged.astype(BF16), wout_ref[...], preferred_element_type=F32)
    xn = _rms(x1, g2_ref[...]).astype(BF16)
    a = jnp.dot(xn, wa_ref[...], preferred_element_type=F32)
    gt = jnp.dot(xn, wgt_ref[...], preferred_element_type=F32)
    act = (a * _sigmoid(a) * gt).astype(BF16)
    x2 = x1 + jnp.dot(act, wo_ref[...], preferred_element_type=F32)
    if final:
        x2 = _rms(x2, fg_ref[...])
    out_ref[...] = x2


def _merge_ffn(x, ys, gates, pw, l, final):
    rows = x.shape[0]
    tm = min(ROW_TILE, rows)
    row_spec = lambda w: pl.BlockSpec((tm, w), lambda i: (i, 0))
    consts = [pw["w_branch"], pw["w_out"], pw["ffn_norm_g"], pw["w_ffn_a"], pw["w_ffn_g"], pw["w_ffn_out"]]
    return pl.pallas_call(
        functools.partial(_merge_ffn_kernel, final),
        grid=(rows // tm,),
        in_specs=([row_spec(D_MODEL), row_spec(3 * MIX), row_spec(GATE_W)] + [_layer_spec(c, l) for c in consts]
                  + [_layer_spec(pw["final_norm_g"], 0)]),
        out_specs=row_spec(D_MODEL),
        out_shape=jax.ShapeDtypeStruct((rows, D_MODEL), F32),
        compiler_params=pltpu.CompilerParams(dimension_semantics=("arbitrary",), vmem_limit_bytes=VMEM_LIMIT),
        name="merge_ffn",
    )(x, ys, gates, *consts, pw["final_norm_g"])


def _att_group(seq_len):
    return 2 if seq_len >= 2 * CHUNK else 1


def _band_bias(rel_bias, lq, grp):
    assert lq - 1 <= MAX_REL
    depth = rel_bias.shape[0]
    rtab = rel_bias[..., ::-1].astype(F32)
    n_far = ATT_WIN + lq - MAX_REL
    ext = jnp.concatenate([jnp.broadcast_to(rtab[..., :1], (depth, ATT_HEADS, n_far)),
                           rtab[..., 1:MAX_REL + lq]], axis=-1)
    wk = ATT_WIN + lq
    period = wk + lq - 1
    rot = jnp.concatenate([ext[..., lq - 1:], ext[..., :lq - 1]], axis=-1)
    chunk = jnp.tile(rot, (1, 1, lq))[..., :lq * (period - 1)].reshape(depth, ATT_HEADS, lq, period - 1)[..., :wk]
    blocks = [jnp.pad(chunk, ((0, 0), (0, 0), (0, 0), (j * lq, (grp - 1 - j) * lq)), constant_values=MASKED)
              for j in range(grp)]
    per_head = jnp.concatenate(blocks, axis=2)
    return per_head.reshape(depth, ATT_HEADS // 2, 2 * grp * lq, ATT_WIN + grp * lq)


def _prep_params(attn_norm_g, w_in, w_gate2, b_gate, gla_norm_g, pool_map, pool_scale, rel_bias, w_branch, w_out,
                 ffn_norm_g, w_ffn_in, w_ffn_out, final_norm_g, prompt_len, sample_len):
    depth = w_in.shape[0]
    o = [0, 512, 768, 1024, 1536, 2048, 2064, 2576, 3088, 3600, 6672]
    col = lambda a, b: w_in[:, :, o[a]:o[b]]
    w_mix = jnp.concatenate([col(0, 1), col(3, 4), col(4, 5), col(6, 7), col(7, 8), col(8, 9)], axis=2)
    w_gla = jnp.concatenate([col(1, 2), col(2, 3), col(5, 6),
                             jnp.zeros((depth, D_MODEL, LANES - GLA_RANK), w_in.dtype)], axis=2)
    wg2 = jnp.concatenate([w_gate2, jnp.zeros((depth, LANES - GLA_RANK, GLA_QK_W), w_gate2.dtype)], axis=1)
    vec = lambda a: a[:, None, :]
    return dict(
        attn_norm_g=vec(attn_norm_g), w_gates=col(9, 10).astype(BF16), w_mix=w_mix.astype(BF16),
        w_gla=w_gla.astype(BF16), w_gate2=wg2.astype(BF16), b_gate=vec(b_gate), gla_norm_g=vec(gla_norm_g),
        pool_map=pool_map.astype(BF16), pool_scale=vec(pool_scale),
        bias_p=_band_bias(rel_bias, CHUNK, _att_group(prompt_len)),
        bias_s=_band_bias(rel_bias, min(CHUNK, sample_len), _att_group(sample_len)),
        w_branch=w_branch.astype(BF16), w_out=w_out.astype(BF16), ffn_norm_g=vec(ffn_norm_g),
        w_ffn_a=w_ffn_in[:, :, :D_FF].astype(BF16), w_ffn_g=w_ffn_in[:, :, D_FF:].astype(BF16),
        w_ffn_out=w_ffn_out.astype(BF16), final_norm_g=final_norm_g[None, None, :])


def _layer(x, n_seq, seq_len, state, pw, l, pos0, final):
    gates, mix, gla = _inproj(x, pw, l)
    if state is not None:
        hist, s0, kc, vc = state
        hist = jnp.pad(hist, ((0, 0), (HIST_ROWS - POOL_HIST, 0), (0, 0)))
        s0t = s0.reshape(n_seq, 2, 2, GLA_DK, GLA_DV).transpose(0, 1, 4, 2, 3).reshape(n_seq, 2, LANES, LANES)
        state = (hist, s0t, kc.reshape(n_seq, ATT_WIN, MIX), vc.reshape(n_seq, ATT_WIN, MIX))
    ys, s_fin_t, pool16, k_new, v_new = _mixers(mix, gla, state, pw, l, n_seq, seq_len, pos0)
    x_new = _merge_ffn(x, ys, gates, pw, l, final)

    keep = min(ATT_WIN, seq_len)
    gla_new = (s_fin_t.reshape(n_seq, 2, GLA_DV, 2, GLA_DK).transpose(0, 1, 3, 4, 2)
               .reshape(n_seq, GLA_HEADS, GLA_DK, GLA_DV))
    return x_new, (pool16[:, HIST_ROWS - POOL_HIST:], gla_new,
                   k_new.reshape(n_seq, keep, ATT_HEADS, ATT_DH), v_new.reshape(n_seq, keep, ATT_HEADS, ATT_DH))


def kernel(x_prompt, x_sample, cache_pool, state_gla, cache_k, cache_v, attn_norm_g, w_in, w_gate2, b_gate,
           gla_norm_g, pool_map, pool_scale, rel_bias, w_branch, w_out, ffn_norm_g, w_ffn_in, w_ffn_out,
           final_norm_g):
    batch, seq, _ = x_prompt.shape
    dec_batch, dec_seq, _ = x_sample.shape
    depth = w_in.shape[0]
    hp = x_prompt.reshape(batch * seq, D_MODEL)
    hs = x_sample.reshape(dec_batch * dec_seq, D_MODEL)
    pw = _prep_params(attn_norm_g, w_in, w_gate2, b_gate, gla_norm_g, pool_map, pool_scale, rel_bias, w_branch,
                      w_out, ffn_norm_g, w_ffn_in, w_ffn_out, final_norm_g, seq, dec_seq)
    outs_p, outs_s = [], []
    for l in range(depth):
        final = l == depth - 1
        hp, sp = _layer(hp, batch, seq, None, pw, l, 0, final)
        hs, ss = _layer(hs, dec_batch, dec_seq, (cache_pool[l], state_gla[l], cache_k[l], cache_v[l]), pw, l,
                        PAST_LEN, final)
        outs_p.append(sp)
        outs_s.append(ss)
    stack = lambda outs, j: jnp.stack([o[j] for o in outs])
    return (hp.reshape(batch, seq, D_MODEL), hs.reshape(dec_batch, dec_seq, D_MODEL),
            stack(outs_p, 0), stack(outs_s, 0), stack(outs_p, 1), stack(outs_s, 1),
            stack(outs_p, 2), stack(outs_s, 2), stack(outs_p, 3), stack(outs_s, 3))
```

```python
import functools

import jax
import jax.numpy as jnp
from jax import lax
from jax.experimental import pallas as pl
from jax.experimental.pallas import tpu as pltpu

F32 = jnp.float32
BF16 = jnp.bfloat16

D_MODEL = 1024
PAST_LEN = 2048
CHUNK = 64
MIX = 512
POOL_WINDOWS = (2, 4, 8, 16)
POOL_GW = 128
POOL_HIST = 15
HIST_ROWS = 16
GLA_HEADS = 4
GLA_DK = 64
GLA_DV = 128
GLA_QK_W = 256
GLA_RANK = 16
GLA_TAU = 16.0
GLA_SAFE_DECAY = 60.0
TOKEN_GROUP = 16
ATT_HEADS = 8
ATT_DH = 64
ATT_WIN = 512
MAX_REL = 128
N_BRANCH = 3
D_FF = 2816
EPS = 1e-6
MASKED = -1e30
LANES = 128

MIX_U, MIX_VB, MIX_RB, MIX_QC, MIX_KC, MIX_VC = 0, 512, 1024, 1536, 2048, 2560
MIX_W = 3072
GLA_Q, GLA_K, GLA_Z = 0, 256, 512
GLA_W = 640
GATE_W = N_BRANCH * D_MODEL
YS_A, YS_B, YS_C = 0, 512, 1024

ROW_TILE = 256
SEQ_TILE = 512
VMEM_LIMIT = 56 * 1024 * 1024

NT_DIMS = (((1,), (1,)), ((), ()))
TN_DIMS = (((0,), (0,)), ((), ()))


def _sigmoid(x):
    return 1.0 / (1.0 + jnp.exp(-x))


def _rms(x, g):
    return x * lax.rsqrt(jnp.mean(x * x, axis=-1, keepdims=True) + EPS) * g


def _layer_spec(arr, l):
    tail = (0,) * (arr.ndim - 1)
    return pl.BlockSpec((None,) + arr.shape[1:], lambda *_: (l,) + tail, pipeline_mode=pl.Buffered(1))


def _inproj_kernel(x_ref, g_ref, wg_ref, wm_ref, wl_ref, og_ref, om_ref, ol_ref):
    xb = _rms(x_ref[...], g_ref[...]).astype(BF16)
    og_ref[...] = jnp.dot(xb, wg_ref[...], preferred_element_type=F32)
    om_ref[...] = jnp.dot(xb, wm_ref[...], preferred_element_type=F32)
    ol_ref[...] = jnp.dot(xb, wl_ref[...], preferred_element_type=F32)


def _inproj(x, pw, l):
    rows = x.shape[0]
    tm = min(ROW_TILE, rows)
    row_spec = lambda w: pl.BlockSpec((tm, w), lambda i: (i, 0))
    consts = [pw["attn_norm_g"], pw["w_gates"], pw["w_mix"], pw["w_gla"]]
    return pl.pallas_call(
        _inproj_kernel,
        grid=(rows // tm,),
        in_specs=[row_spec(D_MODEL)] + [_layer_spec(c, l) for c in consts],
        out_specs=[row_spec(GATE_W), row_spec(MIX_W), row_spec(GLA_W)],
        out_shape=[jax.ShapeDtypeStruct((rows, GATE_W), F32), jax.ShapeDtypeStruct((rows, MIX_W), F32),
                   jax.ShapeDtypeStruct((rows, GLA_W), F32)],
        compiler_params=pltpu.CompilerParams(dimension_semantics=("arbitrary",), vmem_limit_bytes=VMEM_LIMIT),
        name="in_proj",
    )(x, *consts)


def _mixer_kernel(tt, lg, lq, grp, has_state, pos0, *refs):
    refs = list(refs)
    mix_ref, gla_ref = refs[:2]
    k = 2
    if has_state:
        hist_ref, s0_ref, kc_ref, vc_ref = refs[k:k + 4]
        k += 4
    wg2_ref, bg_ref, gng_ref, pmap_ref, pscale_ref, bias_ref = refs[k:k + 6]
    k += 6
    ys_ref, sfin_ref, pool_ref, knew_ref, vnew_ref = refs[k:k + 5]
    carry_ref, st_ref, kbuf_ref, vbuf_ref, la_ref = refs[k + 5:]

    i = pl.program_id(1)
    lane = lax.broadcasted_iota(jnp.int32, (1, LANES), 1)
    head_mask = (lane < ATT_DH, lane >= ATT_DH)

    @pl.when(i == 0)
    def _init():
        if has_state:
            carry_ref[...] = hist_ref[0]
            st_ref[...] = s0_ref[0]
            kbuf_ref[0:ATT_WIN, :] = kc_ref[0].astype(BF16)
            vbuf_ref[0:ATT_WIN, :] = vc_ref[0].astype(BF16)
        else:
            carry_ref[...] = jnp.zeros_like(carry_ref)
            st_ref[...] = jnp.zeros_like(st_ref)
            kbuf_ref[0:ATT_WIN, :] = jnp.zeros((ATT_WIN, MIX), BF16)
            vbuf_ref[0:ATT_WIN, :] = jnp.zeros((ATT_WIN, MIX), BF16)

    u = mix_ref[:, MIX_U:MIX_U + MIX]
    ext = jnp.concatenate([carry_ref[...], u], axis=0)
    carry_ref[...] = u[tt - HIST_ROWS:, :]
    row = lax.broadcasted_iota(jnp.int32, (tt, POOL_GW), 0)
    n_seen = (pos0 + i * tt + row + 1).astype(F32)
    for g, w in enumerate(POOL_WINDOWS):
        cols = slice(g * POOL_GW, (g + 1) * POOL_GW)
        s = ext[:, cols]
        shift = 1
        while shift < w:
            s = s + pltpu.roll(s, shift, 0)
            shift *= 2
        p = s[HIST_ROWS:, :] / jnp.minimum(float(w), n_seen) - u[:, cols]
        y = jnp.dot(p.astype(BF16), pmap_ref[g], preferred_element_type=F32) * pscale_ref[:, cols]
        ys_ref[:, YS_A + g * POOL_GW:YS_A + (g + 1) * POOL_GW] = y.astype(BF16)

    z = gla_ref[:, GLA_Z:GLA_Z + LANES].astype(BF16)
    pre = jnp.dot(z, wg2_ref[...], preferred_element_type=F32) + bg_ref[...]
    log_a = -(jnp.maximum(-pre, 0.0) + jnp.log1p(jnp.exp(-jnp.abs(pre)))) / GLA_TAU
    row_in_blk = lax.broadcasted_iota(jnp.int32, (tt, GLA_QK_W), 0) & (lg - 1)
    b = log_a
    shift = 1
    while shift < lg:
        b = b + jnp.where(row_in_blk >= shift, pltpu.roll(b, shift, 0), 0.0)
        shift *= 2

    ri = lax.broadcasted_iota(jnp.int32, (2 * lg, lg), 0) & (lg - 1)
    ci = lax.broadcasted_iota(jnp.int32, (2 * lg, lg), 1)
    causal = ci <= ri

    def stack_heads(x):
        return jnp.concatenate([jnp.where(head_mask[0], x, 0.0), jnp.where(head_mask[1], x, 0.0)],
                               axis=0).astype(BF16)

    def gla_out(rows, h, o):
        r = mix_ref[rows, MIX_RB + h * GLA_DV:MIX_RB + (h + 1) * GLA_DV]
        y = _rms(o, gng_ref[...]) * (r * _sigmoid(r))
        ys_ref[rows, YS_B + h * GLA_DV:YS_B + (h + 1) * GLA_DV] = y.astype(BF16)

    blocks_safe = jnp.max(-b) <= GLA_SAFE_DECAY

    @pl.when(blocks_safe)
    def _gla_blocks():
        for p in range(GLA_HEADS // 2):
            lanes = slice(p * LANES, (p + 1) * LANES)
            bq = b[:, lanes]
            eb = jnp.exp(bq)
            enb = jnp.exp(-bq)
            qs = gla_ref[:, GLA_Q + p * LANES:GLA_Q + (p + 1) * LANES] * (GLA_DK ** -0.5) * eb
            kt = gla_ref[:, GLA_K + p * LANES:GLA_K + (p + 1) * LANES] * enb
            st = st_ref[p]
            for c in range(tt // lg):
                rows = slice(c * lg, (c + 1) * lg)
                e_last = eb[(c + 1) * lg - 1:(c + 1) * lg, :]
                ktb = kt[rows].astype(BF16)
                kdb = (kt[rows] * e_last).astype(BF16)
                qst = stack_heads(qs[rows])
                att = lax.dot_general(qst, ktb, NT_DIMS, preferred_element_type=F32)
                att = jnp.where(causal, att, 0.0).astype(BF16)
                inter = lax.dot_general(qst, st.astype(BF16), NT_DIMS, preferred_element_type=F32)
                upd = []
                for hl in range(2):
                    h = 2 * p + hl
                    half = slice(hl * lg, (hl + 1) * lg)
                    vh = mix_ref[rows, MIX_VB + h * GLA_DV:MIX_VB + (h + 1) * GLA_DV].astype(BF16)
                    gla_out(rows, h, jnp.dot(att[half], vh, preferred_element_type=F32) + inter[half])
                    upd.append(lax.dot_general(vh, kdb, TN_DIMS, preferred_element_type=F32))
                st = e_last * st + jnp.where(head_mask[0], upd[0], upd[1])
            st_ref[p] = st

    @pl.when(jnp.logical_not(blocks_safe))
    def _gla_tokens():
        la_ref[...] = log_a
        eye = (lax.broadcasted_iota(jnp.int32, (LANES, LANES), 0)
               == lax.broadcasted_iota(jnp.int32, (LANES, LANES), 1)).astype(F32)
        tok = lax.broadcasted_iota(jnp.int32, (TOKEN_GROUP, LANES), 0)
        for p in range(GLA_HEADS // 2):
            def group(gidx, st, p=p):
                rows = pl.ds(pl.multiple_of(gidx * TOKEN_GROUP, TOKEN_GROUP), TOKEN_GROUP)
                a = jnp.exp(la_ref[rows, p * LANES:(p + 1) * LANES])
                q = gla_ref[rows, GLA_Q + p * LANES:GLA_Q + (p + 1) * LANES] * (GLA_DK ** -0.5)
                k = gla_ref[rows, GLA_K + p * LANES:GLA_K + (p + 1) * LANES]
                v_t = [lax.dot_general(eye, mix_ref[rows, MIX_VB + (2 * p + hl) * GLA_DV:
                                                    MIX_VB + (2 * p + hl + 1) * GLA_DV],
                                       NT_DIMS, precision=lax.Precision.HIGHEST, preferred_element_type=F32)
                       for hl in range(2)]
                o = [jnp.zeros((TOKEN_GROUP, GLA_DV), F32), jnp.zeros((TOKEN_GROUP, GLA_DV), F32)]
                for t in range(TOKEN_GROUP):
                    v_col = jnp.where(head_mask[0], v_t[0][:, t:t + 1], v_t[1][:, t:t + 1])
                    st = a[t:t + 1, :] * st + v_col * k[t:t + 1, :]
                    stb = st.astype(BF16)
                    for hl in range(2):
                        q_t = jnp.where(tok == t, jnp.where(head_mask[hl], q, 0.0), 0.0).astype(BF16)
                        o[hl] = o[hl] + lax.dot_general(q_t, stb, NT_DIMS, preferred_element_type=F32)
                for hl in range(2):
                    gla_out(rows, 2 * p + hl, o[hl])
                return st

            st_ref[p] = lax.fori_loop(0, tt // TOKEN_GROUP, group, st_ref[p])

    kbuf_ref[ATT_WIN:ATT_WIN + tt, :] = mix_ref[:, MIX_KC:MIX_KC + MIX].astype(BF16)
    vbuf_ref[ATT_WIN:ATT_WIN + tt, :] = mix_ref[:, MIX_VC:MIX_VC + MIX].astype(BF16)
    gq = grp * lq
    win = ATT_WIN + gq
    kcol = lax.broadcasted_iota(jnp.int32, (1, win), 1)
    first_tile_neg = jnp.where(i == 0, MASKED, 0.0).astype(F32)
    for gi in range(tt // gq):
        qrows = slice(gi * gq, (gi + 1) * gq)
        krows = slice(gi * gq, gi * gq + win)
        if not has_state:
            invalid = jnp.where(kcol < ATT_WIN - gi * gq, first_tile_neg, 0.0)
        for p in range(ATT_HEADS // 2):
            cols = slice(p * LANES, (p + 1) * LANES)
            q2 = mix_ref[qrows, MIX_QC + p * LANES:MIX_QC + (p + 1) * LANES] * (ATT_DH ** -0.5)
            s = lax.dot_general(stack_heads(q2), kbuf_ref[krows, cols], NT_DIMS, preferred_element_type=F32)
            s = s + bias_ref[p]
            if not has_state:
                s = s + invalid
            e = jnp.exp(s - jnp.max(s, axis=-1, keepdims=True))
            den = jnp.sum(e, axis=-1, keepdims=True)
            o = jnp.dot(e.astype(BF16), vbuf_ref[krows, cols], preferred_element_type=F32) / den
            ys_ref[qrows, YS_C + p * LANES:YS_C + (p + 1) * LANES] = (
                jnp.where(head_mask[0], o[:gq], o[gq:]).astype(BF16))

    if not has_state:
        kbuf_ref[0:ATT_WIN, :] = kbuf_ref[tt:tt + ATT_WIN, :]
        vbuf_ref[0:ATT_WIN, :] = vbuf_ref[tt:tt + ATT_WIN, :]

    @pl.when(i == pl.num_programs(1) - 1)
    def _fin():
        keep = min(ATT_WIN, tt)
        sfin_ref[0] = st_ref[...]
        pool_ref[0] = mix_ref[tt - HIST_ROWS:, MIX_U:MIX_U + MIX]
        knew_ref[0] = mix_ref[tt - keep:, MIX_KC:MIX_KC + MIX]
        vnew_ref[0] = mix_ref[tt - keep:, MIX_VC:MIX_VC + MIX]


def _mixers(mix, gla, state, pw, l, n_seq, seq_len, pos0):
    has_state = state is not None
    tt = min(SEQ_TILE, seq_len)
    lg = min(CHUNK, seq_len)
    lq = min(CHUNK, seq_len)
    grp = _att_group(seq_len)
    nt = seq_len // tt
    assert has_state or tt == ATT_WIN
    kernel = functools.partial(_mixer_kernel, tt, lg, lq, grp, has_state, pos0)

    tile = lambda w: pl.BlockSpec((tt, w), lambda s, i: (s * nt + i, 0))
    in_specs = [tile(MIX_W), tile(GLA_W)]
    args = [mix, gla]
    if has_state:
        hist, s0t, kc, vc = state
        in_specs += [pl.BlockSpec((1, HIST_ROWS, MIX), lambda s, i: (s, 0, 0)),
                     pl.BlockSpec((1, 2, LANES, LANES), lambda s, i: (s, 0, 0, 0)),
                     pl.BlockSpec((1, ATT_WIN, MIX), lambda s, i: (s, 0, 0)),
                     pl.BlockSpec((1, ATT_WIN, MIX), lambda s, i: (s, 0, 0))]
        args += [hist, s0t, kc, vc]
    consts = [pw["w_gate2"], pw["b_gate"], pw["gla_norm_g"], pw["pool_map"], pw["pool_scale"],
              pw["bias_s"] if has_state else pw["bias_p"]]
    in_specs += [_layer_spec(c, l) for c in consts]
    args += consts
    keep = min(ATT_WIN, seq_len)
    per_seq = lambda r: pl.BlockSpec((1, r, MIX), lambda s, i: (s, 0, 0))

    return pl.pallas_call(
        kernel,
        grid=(n_seq, nt),
        in_specs=in_specs,
        out_specs=[pl.BlockSpec((tt, 3 * MIX), lambda s, i: (s * nt + i, 0)),
                   pl.BlockSpec((1, 2, LANES, LANES), lambda s, i: (s, 0, 0, 0)),
                   per_seq(HIST_ROWS), per_seq(keep), per_seq(keep)],
        out_shape=[jax.ShapeDtypeStruct((n_seq * seq_len, 3 * MIX), BF16),
                   jax.ShapeDtypeStruct((n_seq, 2, LANES, LANES), F32),
                   jax.ShapeDtypeStruct((n_seq, HIST_ROWS, MIX), F32),
                   jax.ShapeDtypeStruct((n_seq, keep, MIX), F32),
                   jax.ShapeDtypeStruct((n_seq, keep, MIX), F32)],
        scratch_shapes=[pltpu.VMEM((HIST_ROWS, MIX), F32),
                        pltpu.VMEM((2, LANES, LANES), F32),
                        pltpu.VMEM((ATT_WIN + tt, MIX), BF16),
                        pltpu.VMEM((ATT_WIN + tt, MIX), BF16),
                        pltpu.VMEM((tt, GLA_QK_W), F32)],
        compiler_params=pltpu.CompilerParams(dimension_semantics=("arbitrary", "arbitrary"),
                                             vmem_limit_bytes=VMEM_LIMIT),
        name="mixers_sample" if has_state else "mixers_prompt",
    )(*args)


def _merge_ffn_kernel(final, x_ref, ys_ref, hg_ref, wbr_ref, wout_ref, g2_ref, wa_ref, wgt_ref, wo_ref, fg_ref,
                      out_ref):
    merged = None
    for g in range(N_BRANCH):
        branch = jnp.dot(ys_ref[:, g * MIX:(g + 1) * MIX], wbr_ref[g], preferred_element_type=F32)
        term = _sigmoid(hg_ref[:, g * D_MODEL:(g + 1) * D_MODEL]) * branch
        merged = term if merged is None else merged + term
    x1 = x_ref[...] + jnp.dot(merged.astype(BF16), wout_ref[...], preferred_element_type=F32)
    xn = _rms(x1, g2_ref[...]).astype(BF16)
    a = jnp.dot(xn, wa_ref[...], preferred_element_type=F32)
    gt = jnp.dot(xn, wgt_ref[...], preferred_element_type=F32)
    act = (a * _sigmoid(a) * gt).astype(BF16)
    x2 = x1 + jnp.dot(act, wo_ref[...], preferred_element_type=F32)
    if final:
        x2 = _rms(x2, fg_ref[...])
    out_ref[...] = x2


def _merge_ffn(x, ys, gates, pw, l, final):
    rows = x.shape[0]
    tm = min(ROW_TILE, rows)
    row_spec = lambda w: pl.BlockSpec((tm, w), lambda i: (i, 0))
    consts = [pw["w_branch"], pw["w_out"], pw["ffn_norm_g"], pw["w_ffn_a"], pw["w_ffn_g"], pw["w_ffn_out"]]
    return pl.pallas_call(
        functools.partial(_merge_ffn_kernel, final),
        grid=(rows // tm,),
        in_specs=([row_spec(D_MODEL), row_spec(3 * MIX), row_spec(GATE_W)] + [_layer_spec(c, l) for c in consts]
                  + [_layer_spec(pw["final_norm_g"], 0)]),
        out_specs=row_spec(D_MODEL),
        out_shape=jax.ShapeDtypeStruct((rows, D_MODEL), F32),
        compiler_params=pltpu.CompilerParams(dimension_semantics=("arbitrary",), vmem_limit_bytes=VMEM_LIMIT),
        name="merge_ffn",
    )(x, ys, gates, *consts, pw["final_norm_g"])


def _att_group(seq_len):
    return 2 if seq_len >= 2 * CHUNK else 1


def _band_bias(rel_bias, lq, grp):
    assert lq - 1 <= MAX_REL
    depth = rel_bias.shape[0]
    rtab = rel_bias[..., ::-1].astype(F32)
    n_far = ATT_WIN + lq - MAX_REL
    ext = jnp.concatenate([jnp.broadcast_to(rtab[..., :1], (depth, ATT_HEADS, n_far)),
                           rtab[..., 1:MAX_REL + lq]], axis=-1)
    wk = ATT_WIN + lq
    period = wk + lq - 1
    rot = jnp.concatenate([ext[..., lq - 1:], ext[..., :lq - 1]], axis=-1)
    chunk = jnp.tile(rot, (1, 1, lq))[..., :lq * (period - 1)].reshape(depth, ATT_HEADS, lq, period - 1)[..., :wk]
    blocks = [jnp.pad(chunk, ((0, 0), (0, 0), (0, 0), (j * lq, (grp - 1 - j) * lq)), constant_values=MASKED)
              for j in range(grp)]
    per_head = jnp.concatenate(blocks, axis=2)
    return per_head.reshape(depth, ATT_HEADS // 2, 2 * grp * lq, ATT_WIN + grp * lq)


def _prep_params(attn_norm_g, w_in, w_gate2, b_gate, gla_norm_g, pool_map, pool_scale, rel_bias, w_branch, w_out,
                 ffn_norm_g, w_ffn_in, w_ffn_out, final_norm_g, prompt_len, sample_len):
    depth = w_in.shape[0]
    o = [0, 512, 768, 1024, 1536, 2048, 2064, 2576, 3088, 3600, 6672]
    col = lambda a, b: w_in[:, :, o[a]:o[b]]
    w_mix = jnp.concatenate([col(0, 1), col(3, 4), col(4, 5), col(6, 7), col(7, 8), col(8, 9)], axis=2)
    w_gla = jnp.concatenate([col(1, 2), col(2, 3), col(5, 6),
                             jnp.zeros((depth, D_MODEL, LANES - GLA_RANK), w_in.dtype)], axis=2)
    wg2 = jnp.concatenate([w_gate2, jnp.zeros((depth, LANES - GLA_RANK, GLA_QK_W), w_gate2.dtype)], axis=1)
    vec = lambda a: a[:, None, :]
    return dict(
        attn_norm_g=vec(attn_norm_g), w_gates=col(9, 10).astype(BF16), w_mix=w_mix.astype(BF16),
        w_gla=w_gla.astype(BF16), w_gate2=wg2.astype(BF16), b_gate=vec(b_gate), gla_norm_g=vec(gla_norm_g),
        pool_map=pool_map.astype(BF16), pool_scale=vec(pool_scale),
        bias_p=_band_bias(rel_bias, CHUNK, _att_group(prompt_len)),
        bias_s=_band_bias(rel_bias, min(CHUNK, sample_len), _att_group(sample_len)),
        w_branch=w_branch.astype(BF16), w_out=w_out.astype(BF16), ffn_norm_g=vec(ffn_norm_g),
        w_ffn_a=w_ffn_in[:, :, :D_FF].astype(BF16), w_ffn_g=w_ffn_in[:, :, D_FF:].astype(BF16),
        w_ffn_out=w_ffn_out.astype(BF16), final_norm_g=final_norm_g[None, None, :])


def _layer(x, n_seq, seq_len, state, pw, l, pos0, final):
    gates, mix, gla = _inproj(x, pw, l)
    if state is not None:
        hist, s0, kc, vc = state
        hist = jnp.pad(hist, ((0, 0), (HIST_ROWS - POOL_HIST, 0), (0, 0)))
        s0t = s0.reshape(n_seq, 2, 2, GLA_DK, GLA_DV).transpose(0, 1, 4, 2, 3).reshape(n_seq, 2, LANES, LANES)
        state = (hist, s0t, kc.reshape(n_seq, ATT_WIN, MIX), vc.reshape(n_seq, ATT_WIN, MIX))
    ys, s_fin_t, pool16, k_new, v_new = _mixers(mix, gla, state, pw, l, n_seq, seq_len, pos0)
    x_new = _merge_ffn(x, ys, gates, pw, l, final)

    keep = min(ATT_WIN, seq_len)
    gla_new = (s_fin_t.reshape(n_seq, 2, GLA_DV, 2, GLA_DK).transpose(0, 1, 3, 4, 2)
               .reshape(n_seq, GLA_HEADS, GLA_DK, GLA_DV))
    return x_new, (pool16[:, HIST_ROWS - POOL_HIST:], gla_new,
                   k_new.reshape(n_seq, keep, ATT_HEADS, ATT_DH), v_new.reshape(n_seq, keep, ATT_HEADS, ATT_DH))


def kernel(x_prompt, x_sample, cache_pool, state_gla, cache_k, cache_v, attn_norm_g, w_in, w_gate2, b_gate,
           gla_norm_g, pool_map, pool_scale, rel_bias, w_branch, w_out, ffn_norm_g, w_ffn_in, w_ffn_out,
           final_norm_g):
    batch, seq, _ = x_prompt.shape
    dec_batch, dec_seq, _ = x_sample.shape
    depth = w_in.shape[0]
    hp = x_prompt.reshape(batch * seq, D_MODEL)
    hs = x_sample.reshape(dec_batch * dec_seq, D_MODEL)
    pw = _prep_params(attn_norm_g, w_in, w_gate2, b_gate, gla_norm_g, pool_map, pool_scale, rel_bias, w_branch,
                      w_out, ffn_norm_g, w_ffn_in, w_ffn_out, final_norm_g, seq, dec_seq)
    outs_p, outs_s = [], []
    for l in range(depth):
        final = l == depth - 1
        hp, sp = _layer(hp, batch, seq, None, pw, l, 0, final)
        hs, ss = _layer(hs, dec_batch, dec_seq, (cache_pool[l], state_gla[l], cache_k[l], cache_v[l]), pw, l,
                        PAST_LEN, final)
        outs_p.append(sp)
        outs_s.append(ss)
    stack = lambda outs, j: jnp.stack([o[j] for o in outs])
    return (hp.reshape(batch, seq, D_MODEL), hs.reshape(dec_batch, dec_seq, D_MODEL),
            stack(outs_p, 0), stack(outs_s, 0), stack(outs_p, 1), stack(outs_s, 1),
            stack(outs_p, 2), stack(outs_s, 2), stack(outs_p, 3), stack(outs_s, 3))
```

```python
import functools

import jax
import jax.numpy as jnp
from jax import lax
from jax.experimental import pallas as pl
from jax.experimental.pallas import tpu as pltpu

F32 = jnp.float32
BF16 = jnp.bfloat16

D_MODEL = 1024
PAST_LEN = 2048
CHUNK = 64
MIX = 512
POOL_WINDOWS = (2, 4, 8, 16)
POOL_GW = 128
POOL_HIST = 15
HIST_ROWS = 16
GLA_HEADS = 4
GLA_DK = 64
GLA_DV = 128
GLA_QK_W = 256
GLA_RANK = 16
GLA_TAU = 16.0
GLA_SAFE_DECAY = 60.0
TOKEN_GROUP = 16
ATT_HEADS = 8
ATT_DH = 64
ATT_WIN = 512
MAX_REL = 128
N_BRANCH = 3
D_FF = 2816
EPS = 1e-6
MASKED = -1e30
LANES = 128

MIX_U, MIX_QB, MIX_KB, MIX_VB, MIX_RB, MIX_QC, MIX_KC, MIX_VC = 0, 512, 768, 1024, 1536, 2048, 2560, 3072
MIX_W = 3584
GATE_W = N_BRANCH * D_MODEL
W_MIX, W_GATES, W_Z = 0, MIX_W, MIX_W + GATE_W
W_IN_COLS = W_Z + LANES
YS_A, YS_B, YS_C = 0, 512, 1024

ROW_TILE = 256
SEQ_TILE = 512
VMEM_LIMIT = 56 * 1024 * 1024

NT_DIMS = (((1,), (1,)), ((), ()))
TN_DIMS = (((0,), (0,)), ((), ()))


def _sigmoid(x):
    return 1.0 / (1.0 + jnp.exp(-x))


def _rms(x, g):
    return x * lax.rsqrt(jnp.mean(x * x, axis=-1, keepdims=True) + EPS) * g


def _layer_spec(arr, l):
    tail = (0,) * (arr.ndim - 1)
    return pl.BlockSpec((None,) + arr.shape[1:], lambda *_: (l,) + tail, pipeline_mode=pl.Buffered(1))


def _inproj_kernel(x_ref, g_ref, w_ref, om_ref, og_ref, oz_ref):
    xb = _rms(x_ref[...], g_ref[...]).astype(BF16)
    om_ref[...] = jnp.dot(xb, w_ref[:, W_MIX:W_MIX + MIX_W], preferred_element_type=F32)
    og_ref[...] = jnp.dot(xb, w_ref[:, W_GATES:W_GATES + GATE_W], preferred_element_type=F32)
    oz_ref[...] = jnp.dot(xb, w_ref[:, W_Z:W_Z + LANES], preferred_element_type=F32)


def _inproj(x, pw, l):
    rows = x.shape[0]
    tm = min(ROW_TILE, rows)
    row_spec = lambda w: pl.BlockSpec((tm, w), lambda i: (i, 0))
    consts = [pw["attn_norm_g"], pw["w_in"]]
    return pl.pallas_call(
        _inproj_kernel,
        grid=(rows // tm,),
        in_specs=[row_spec(D_MODEL)] + [_layer_spec(c, l) for c in consts],
        out_specs=[row_spec(MIX_W), row_spec(GATE_W), row_spec(LANES)],
        out_shape=[jax.ShapeDtypeStruct((rows, MIX_W), F32), jax.ShapeDtypeStruct((rows, GATE_W), F32),
                   jax.ShapeDtypeStruct((rows, LANES), F32)],
        compiler_params=pltpu.CompilerParams(dimension_semantics=("arbitrary",), vmem_limit_bytes=VMEM_LIMIT),
        name="in_proj",
    )(x, *consts)


def _mixer_kernel(tt, lg, lq, grp, has_state, pos0, *refs):
    refs = list(refs)
    mix_ref, z_ref = refs[:2]
    k = 2
    if has_state:
        hist_ref, s0_ref, kc_ref, vc_ref = refs[k:k + 4]
        k += 4
    wg2_ref, bg_ref, gng_ref, pmap_ref, pscale_ref, bias_ref = refs[k:k + 6]
    k += 6
    ys_ref, sfin_ref, pool_ref, knew_ref, vnew_ref = refs[k:k + 5]
    carry_ref, st_ref, kbuf_ref, vbuf_ref, la_ref = refs[k + 5:]

    i = pl.program_id(1)
    lane = lax.broadcasted_iota(jnp.int32, (1, LANES), 1)
    head_mask = (lane < ATT_DH, lane >= ATT_DH)

    @pl.when(i == 0)
    def _init():
        if has_state:
            carry_ref[...] = hist_ref[0]
            st_ref[...] = s0_ref[0]
            kbuf_ref[0:ATT_WIN, :] = kc_ref[0].astype(BF16)
            vbuf_ref[0:ATT_WIN, :] = vc_ref[0].astype(BF16)
        else:
            carry_ref[...] = jnp.zeros_like(carry_ref)
            st_ref[...] = jnp.zeros_like(st_ref)
            kbuf_ref[0:ATT_WIN, :] = jnp.zeros((ATT_WIN, MIX), BF16)
            vbuf_ref[0:ATT_WIN, :] = jnp.zeros((ATT_WIN, MIX), BF16)

    u = mix_ref[:, MIX_U:MIX_U + MIX]
    ext = jnp.concatenate([carry_ref[...], u], axis=0)
    carry_ref[...] = u[tt - HIST_ROWS:, :]
    row = lax.broadcasted_iota(jnp.int32, (tt, POOL_GW), 0)
    n_seen = (pos0 + i * tt + row + 1).astype(F32)
    for g, w in enumerate(POOL_WINDOWS):
        cols = slice(g * POOL_GW, (g + 1) * POOL_GW)
        s = ext[:, cols]
        shift = 1
        while shift < w:
            s = s + pltpu.roll(s, shift, 0)
            shift *= 2
        p = s[HIST_ROWS:, :] / jnp.minimum(float(w), n_seen) - u[:, cols]
        y = jnp.dot(p.astype(BF16), pmap_ref[g], preferred_element_type=F32) * pscale_ref[:, cols]
        ys_ref[:, YS_A + g * POOL_GW:YS_A + (g + 1) * POOL_GW] = y.astype(BF16)

    z = z_ref[...].astype(BF16)
    pre = jnp.dot(z, wg2_ref[...], preferred_element_type=F32) + bg_ref[...]
    log_a = -(jnp.maximum(-pre, 0.0) + jnp.log1p(jnp.exp(-jnp.abs(pre)))) / GLA_TAU
    row_in_blk = lax.broadcasted_iota(jnp.int32, (tt, GLA_QK_W), 0) & (lg - 1)
    b = log_a
    shift = 1
    while shift < lg:
        b = b + jnp.where(row_in_blk >= shift, pltpu.roll(b, shift, 0), 0.0)
        shift *= 2

    ri = lax.broadcasted_iota(jnp.int32, (2 * lg, lg), 0) & (lg - 1)
    ci = lax.broadcasted_iota(jnp.int32, (2 * lg, lg), 1)
    causal = ci <= ri

    def stack_heads(x):
        return jnp.concatenate([jnp.where(head_mask[0], x, 0.0), jnp.where(head_mask[1], x, 0.0)],
                               axis=0).astype(BF16)

    def gla_out(rows, h, o):
        r = mix_ref[rows, MIX_RB + h * GLA_DV:MIX_RB + (h + 1) * GLA_DV]
        y = _rms(o, gng_ref[...]) * (r * _sigmoid(r))
        ys_ref[rows, YS_B + h * GLA_DV:YS_B + (h + 1) * GLA_DV] = y.astype(BF16)

    blocks_safe = jnp.max(-b) <= GLA_SAFE_DECAY

    @pl.when(blocks_safe)
    def _gla_blocks():
        for p in range(GLA_HEADS // 2):
            lanes = slice(p * LANES, (p + 1) * LANES)
            bq = b[:, lanes]
            eb = jnp.exp(bq)
            enb = jnp.exp(-bq)
            qs = mix_ref[:, MIX_QB + p * LANES:MIX_QB + (p + 1) * LANES] * (GLA_DK ** -0.5) * eb
            kt = mix_ref[:, MIX_KB + p * LANES:MIX_KB + (p + 1) * LANES] * enb
            st = st_ref[p]
            for c in range(tt // lg):
                rows = slice(c * lg, (c + 1) * lg)
                e_last = eb[(c + 1) * lg - 1:(c + 1) * lg, :]
                ktb = kt[rows].astype(BF16)
                kdb = (kt[rows] * e_last).astype(BF16)
                qst = stack_heads(qs[rows])
                att = lax.dot_general(qst, ktb, NT_DIMS, preferred_element_type=F32)
                att = jnp.where(causal, att, 0.0).astype(BF16)
                inter = lax.dot_general(qst, st.astype(BF16), NT_DIMS, preferred_element_type=F32)
                upd = []
                for hl in range(2):
                    h = 2 * p + hl
                    half = slice(hl * lg, (hl + 1) * lg)
                    vh = mix_ref[rows, MIX_VB + h * GLA_DV:MIX_VB + (h + 1) * GLA_DV].astype(BF16)
                    gla_out(rows, h, jnp.dot(att[half], vh, preferred_element_type=F32) + inter[half])
                    upd.append(lax.dot_general(vh, kdb, TN_DIMS, preferred_element_type=F32))
                st = e_last * st + jnp.where(head_mask[0], upd[0], upd[1])
            st_ref[p] = st

    @pl.when(jnp.logical_not(blocks_safe))
    def _gla_tokens():
        la_ref[...] = log_a
        eye = (lax.broadcasted_iota(jnp.int32, (LANES, LANES), 0)
               == lax.broadcasted_iota(jnp.int32, (LANES, LANES), 1)).astype(F32)
        tok = lax.broadcasted_iota(jnp.int32, (TOKEN_GROUP, LANES), 0)
        for p in range(GLA_HEADS // 2):
            def group(gidx, st, p=p):
                rows = pl.ds(pl.multiple_of(gidx * TOKEN_GROUP, TOKEN_GROUP), TOKEN_GROUP)
                a = jnp.exp(la_ref[rows, p * LANES:(p + 1) * LANES])
                q = mix_ref[rows, MIX_QB + p * LANES:MIX_QB + (p + 1) * LANES] * (GLA_DK ** -0.5)
                k = mix_ref[rows, MIX_KB + p * LANES:MIX_KB + (p + 1) * LANES]
                v_t = [lax.dot_general(eye, mix_ref[rows, MIX_VB + (2 * p + hl) * GLA_DV:
                                                    MIX_VB + (2 * p + hl + 1) * GLA_DV],
                                       NT_DIMS, precision=lax.Precision.HIGHEST, preferred_element_type=F32)
                       for hl in range(2)]
                o = [jnp.zeros((TOKEN_GROUP, GLA_DV), F32), jnp.zeros((TOKEN_GROUP, GLA_DV), F32)]
                for t in range(TOKEN_GROUP):
                    v_col = jnp.where(head_mask[0], v_t[0][:, t:t + 1], v_t[1][:, t:t + 1])
                    st = a[t:t + 1, :] * st + v_col * k[t:t + 1, :]
                    stb = st.astype(BF16)
                    for hl in range(2):
                        q_t = jnp.where(tok == t, jnp.where(head_mask[hl], q, 0.0), 0.0).astype(BF16)
                        o[hl] = o[hl] + lax.dot_general(q_t, stb, NT_DIMS, preferred_element_type=F32)
                for hl in range(2):
                    gla_out(rows, 2 * p + hl, o[hl])
                return st

            st_ref[p] = lax.fori_loop(0, tt // TOKEN_GROUP, group, st_ref[p])

    kbuf_ref[ATT_WIN:ATT_WIN + tt, :] = mix_ref[:, MIX_KC:MIX_KC + MIX].astype(BF16)
    vbuf_ref[ATT_WIN:ATT_WIN + tt, :] = mix_ref[:, MIX_VC:MIX_VC + MIX].astype(BF16)
    gq = grp * lq
    win = ATT_WIN + gq
    kcol = lax.broadcasted_iota(jnp.int32, (1, win), 1)
    first_tile_neg = jnp.where(i == 0, MASKED, 0.0).astype(F32)
    for gi in range(tt // gq):
        qrows = slice(gi * gq, (gi + 1) * gq)
        krows = slice(gi * gq, gi * gq + win)
        if not has_state:
            invalid = jnp.where(kcol < ATT_WIN - gi * gq, first_tile_neg, 0.0)
        for p in range(ATT_HEADS // 2):
            cols = slice(p * LANES, (p + 1) * LANES)
            q2 = mix_ref[qrows, MIX_QC + p * LANES:MIX_QC + (p + 1) * LANES] * (ATT_DH ** -0.5)
            s = lax.dot_general(stack_heads(q2), kbuf_ref[krows, cols], NT_DIMS, preferred_element_type=F32)
            s = s + bias_ref[p]
            if not has_state:
                s = s + invalid
            e = jnp.exp(s - jnp.max(s, axis=-1, keepdims=True))
            den = jnp.sum(e, axis=-1, keepdims=True)
            o = jnp.dot(e.astype(BF16), vbuf_ref[krows, cols], preferred_element_type=F32) / den
            ys_ref[qrows, YS_C + p * LANES:YS_C + (p + 1) * LANES] = (
                jnp.where(head_mask[0], o[:gq], o[gq:]).astype(BF16))

    if not has_state:
        kbuf_ref[0:ATT_WIN, :] = kbuf_ref[tt:tt + ATT_WIN, :]
        vbuf_ref[0:ATT_WIN, :] = vbuf_ref[tt:tt + ATT_WIN, :]

    @pl.when(i == pl.num_programs(1) - 1)
    def _fin():
        keep = min(ATT_WIN, tt)
        sfin_ref[0] = st_ref[...]
        pool_ref[0] = mix_ref[tt - HIST_ROWS:, MIX_U:MIX_U + MIX]
        knew_ref[0] = mix_ref[tt - keep:, MIX_KC:MIX_KC + MIX]
        vnew_ref[0] = mix_ref[tt - keep:, MIX_VC:MIX_VC + MIX]


def _mixers(mix, z, state, pw, l, n_seq, seq_len, pos0):
    has_state = state is not None
    tt = min(SEQ_TILE, seq_len)
    lg = min(CHUNK, seq_len)
    lq = min(CHUNK, seq_len)
    grp = _att_group(seq_len)
    nt = seq_len // tt
    assert has_state or tt == ATT_WIN
    kernel = functools.partial(_mixer_kernel, tt, lg, lq, grp, has_state, pos0)

    tile = lambda w: pl.BlockSpec((tt, w), lambda s, i: (s * nt + i, 0))
    in_specs = [tile(MIX_W), tile(LANES)]
    args = [mix, z]
    if has_state:
        hist, s0t, kc, vc = state
        in_specs += [pl.BlockSpec((1, HIST_ROWS, MIX), lambda s, i: (s, 0, 0)),
                     pl.BlockSpec((1, 2, LANES, LANES), lambda s, i: (s, 0, 0, 0)),
                     pl.BlockSpec((1, ATT_WIN, MIX), lambda s, i: (s, 0, 0)),
                     pl.BlockSpec((1, ATT_WIN, MIX), lambda s, i: (s, 0, 0))]
        args += [hist, s0t, kc, vc]
    consts = [pw["w_gate2"], pw["b_gate"], pw["gla_norm_g"], pw["pool_map"], pw["pool_scale"],
              pw["bias_s"] if has_state else pw["bias_p"]]
    in_specs += [_layer_spec(c, l) for c in consts]
    args += consts
    keep = min(ATT_WIN, seq_len)
    per_seq = lambda r: pl.BlockSpec((1, r, MIX), lambda s, i: (s, 0, 0))

    return pl.pallas_call(
        kernel,
        grid=(n_seq, nt),
        in_specs=in_specs,
        out_specs=[pl.BlockSpec((tt, 3 * MIX), lambda s, i: (s * nt + i, 0)),
                   pl.BlockSpec((1, 2, LANES, LANES), lambda s, i: (s, 0, 0, 0)),
                   per_seq(HIST_ROWS), per_seq(keep), per_seq(keep)],
        out_shape=[jax.ShapeDtypeStruct((n_seq * seq_len, 3 * MIX), BF16),
                   jax.ShapeDtypeStruct((n_seq, 2, LANES, LANES), F32),
                   jax.ShapeDtypeStruct((n_seq, HIST_ROWS, MIX), F32),
                   jax.ShapeDtypeStruct((n_seq, keep, MIX), F32),
                   jax.ShapeDtypeStruct((n_seq, keep, MIX), F32)],
        scratch_shapes=[pltpu.VMEM((HIST_ROWS, MIX), F32),
                        pltpu.VMEM((2, LANES, LANES), F32),
                        pltpu.VMEM((ATT_WIN + tt, MIX), BF16),
                        pltpu.VMEM((ATT_WIN + tt, MIX), BF16),
                        pltpu.VMEM((tt, GLA_QK_W), F32)],
        compiler_params=pltpu.CompilerParams(dimension_semantics=("arbitrary", "arbitrary"),
                                             vmem_limit_bytes=VMEM_LIMIT),
        name="mixers_sample" if has_state else "mixers_prompt",
    )(*args)


def _merge_ffn_kernel(final, x_ref, ys_ref, hg_ref, wbr_ref, wout_ref, g2_ref, wf_ref, wo_ref, fg_ref, out_ref):
    merged = None
    for g in range(N_BRANCH):
        branch = jnp.dot(ys_ref[:, g * MIX:(g + 1) * MIX], wbr_ref[g], preferred_element_type=F32)
        term = _sigmoid(hg_ref[:, g * D_MODEL:(g + 1) * D_MODEL]) * branch
        merged = term if merged is None else merged + term
    x1 = x_ref[...] + jnp.dot(merged.astype(BF16), wout_ref[...], preferred_element_type=F32)
    xn = _rms(x1, g2_ref[...]).astype(BF16)
    a = jnp.dot(xn, wf_ref[:, :D_FF], preferred_element_type=F32)
    gt = jnp.dot(xn, wf_ref[:, D_FF:], preferred_element_type=F32)
    act = (a * _sigmoid(a) * gt).astype(BF16)
    x2 = x1 + jnp.dot(act, wo_ref[...], preferred_element_type=F32)
    if final:
        x2 = _rms(x2, fg_ref[...])
    out_ref[...] = x2


def _merge_ffn(x, ys, gates, pw, l, final):
    rows = x.shape[0]
    tm = min(ROW_TILE, rows)
    row_spec = lambda w: pl.BlockSpec((tm, w), lambda i: (i, 0))
    consts = [pw["w_branch"], pw["w_out"], pw["ffn_norm_g"], pw["w_ffn_in"], pw["w_ffn_out"]]
    return pl.pallas_call(
        functools.partial(_merge_ffn_kernel, final),
        grid=(rows // tm,),
        in_specs=([row_spec(D_MODEL), row_spec(3 * MIX), row_spec(GATE_W)] + [_layer_spec(c, l) for c in consts]
                  + [_layer_spec(pw["final_norm_g"], 0)]),
        out_specs=row_spec(D_MODEL),
        out_shape=jax.ShapeDtypeStruct((rows, D_MODEL), F32),
        compiler_params=pltpu.CompilerParams(dimension_semantics=("arbitrary",), vmem_limit_bytes=VMEM_LIMIT),
        name="merge_ffn",
    )(x, ys, gates, *consts, pw["final_norm_g"])


def _att_group(seq_len):
    return 2 if seq_len >= 2 * CHUNK else 1


def _band_bias(rel_bias, lq, grp):
    assert lq - 1 <= MAX_REL
    depth = rel_bias.shape[0]
    rtab = rel_bias[..., ::-1].astype(F32)
    n_far = ATT_WIN + lq - MAX_REL
    ext = jnp.concatenate([jnp.broadcast_to(rtab[..., :1], (depth, ATT_HEADS, n_far)),
                           rtab[..., 1:MAX_REL + lq]], axis=-1)
    wk = ATT_WIN + lq
    period = wk + lq - 1
    rot = jnp.concatenate([ext[..., lq - 1:], ext[..., :lq - 1]], axis=-1)
    chunk = jnp.tile(rot, (1, 1, lq))[..., :lq * (period - 1)].reshape(depth, ATT_HEADS, lq, period - 1)[..., :wk]
    blocks = [jnp.pad(chunk, ((0, 0), (0, 0), (0, 0), (j * lq, (grp - 1 - j) * lq)), constant_values=MASKED)
              for j in range(grp)]
    per_head = jnp.concatenate(blocks, axis=2)
    return per_head.reshape(depth, ATT_HEADS // 2, 2 * grp * lq, ATT_WIN + grp * lq)


def _prep_params(attn_norm_g, w_in, w_gate2, b_gate, gla_norm_g, pool_map, pool_scale, rel_bias, w_branch, w_out,
                 ffn_norm_g, w_ffn_in, w_ffn_out, final_norm_g, prompt_len, sample_len):
    depth = w_in.shape[0]
    z0 = MIX_U + MIX + 2 * GLA_QK_W + 2 * MIX
    w_in_b = jnp.concatenate([w_in[:, :, :z0], w_in[:, :, z0 + GLA_RANK:], w_in[:, :, z0:z0 + GLA_RANK],
                              jnp.zeros((depth, D_MODEL, LANES - GLA_RANK), w_in.dtype)], axis=2).astype(BF16)
    assert w_in_b.shape[2] == W_IN_COLS
    wg2 = jnp.concatenate([w_gate2, jnp.zeros((depth, LANES - GLA_RANK, GLA_QK_W), w_gate2.dtype)], axis=1)
    vec = lambda a: a[:, None, :]
    return dict(
        attn_norm_g=vec(attn_norm_g), w_in=w_in_b, w_gate2=wg2.astype(BF16), b_gate=vec(b_gate),
        gla_norm_g=vec(gla_norm_g), pool_map=pool_map.astype(BF16), pool_scale=vec(pool_scale),
        bias_p=_band_bias(rel_bias, CHUNK, _att_group(prompt_len)),
        bias_s=_band_bias(rel_bias, min(CHUNK, sample_len), _att_group(sample_len)),
        w_branch=w_branch.astype(BF16), w_out=w_out.astype(BF16), ffn_norm_g=vec(ffn_norm_g),
        w_ffn_in=w_ffn_in.astype(BF16), w_ffn_out=w_ffn_out.astype(BF16),
        final_norm_g=final_norm_g[None, None, :])


def _layer(x, n_seq, seq_len, state, pw, l, pos0, final):
    mix, gates, z = _inproj(x, pw, l)
    if state is not None:
        hist, s0, kc, vc = state
        hist = jnp.pad(hist, ((0, 0), (HIST_ROWS - POOL_HIST, 0), (0, 0)))
        s0t = s0.reshape(n_seq, 2, 2, GLA_DK, GLA_DV).transpose(0, 1, 4, 2, 3).reshape(n_seq, 2, LANES, LANES)
        state = (hist, s0t, kc.reshape(n_seq, ATT_WIN, MIX), vc.reshape(n_seq, ATT_WIN, MIX))
    ys, s_fin_t, pool16, k_new, v_new = _mixers(mix, z, state, pw, l, n_seq, seq_len, pos0)
    x_new = _merge_ffn(x, ys, gates, pw, l, final)

    keep = min(ATT_WIN, seq_len)
    gla_new = (s_fin_t.reshape(n_seq, 2, GLA_DV, 2, GLA_DK).transpose(0, 1, 3, 4, 2)
               .reshape(n_seq, GLA_HEADS, GLA_DK, GLA_DV))
    return x_new, (pool16[:, HIST_ROWS - POOL_HIST:], gla_new,
                   k_new.reshape(n_seq, keep, ATT_HEADS, ATT_DH), v_new.reshape(n_seq, keep, ATT_HEADS, ATT_DH))


def kernel(x_prompt, x_sample, cache_pool, state_gla, cache_k, cache_v, attn_norm_g, w_in, w_gate2, b_gate,
           gla_norm_g, pool_map, pool_scale, rel_bias, w_branch, w_out, ffn_norm_g, w_ffn_in, w_ffn_out,
           final_norm_g):
    batch, seq, _ = x_prompt.shape
    dec_batch, dec_seq, _ = x_sample.shape
    depth = w_in.shape[0]
    hp = x_prompt.reshape(batch * seq, D_MODEL)
    hs = x_sample.reshape(dec_batch * dec_seq, D_MODEL)
    pw = _prep_params(attn_norm_g, w_in, w_gate2, b_gate, gla_norm_g, pool_map, pool_scale, rel_bias, w_branch,
                      w_out, ffn_norm_g, w_ffn_in, w_ffn_out, final_norm_g, seq, dec_seq)
    outs_p, outs_s = [], []
    for l in range(depth):
        final = l == depth - 1
        hp, sp = _layer(hp, batch, seq, None, pw, l, 0, final)
        hs, ss = _layer(hs, dec_batch, dec_seq, (cache_pool[l], state_gla[l], cache_k[l], cache_v[l]), pw, l,
                        PAST_LEN, final)
        outs_p.append(sp)
        outs_s.append(ss)
    stack = lambda outs, j: jnp.stack([o[j] for o in outs])
    return (hp.reshape(batch, seq, D_MODEL), hs.reshape(dec_batch, dec_seq, D_MODEL),
            stack(outs_p, 0), stack(outs_s, 0), stack(outs_p, 1), stack(outs_s, 1),
            stack(outs_p, 2), stack(outs_s, 2), stack(outs_p, 3), stack(outs_s, 3))
```

```python
import functools

import jax
import jax.numpy as jnp
from jax import lax
from jax.experimental import pallas as pl
from jax.experimental.pallas import tpu as pltpu

F32 = jnp.float32
BF16 = jnp.bfloat16

D_MODEL = 1024
PAST_LEN = 2048
CHUNK = 64
MIX = 512
POOL_WINDOWS = (2, 4, 8, 16)
POOL_GW = 128
POOL_HIST = 15
HIST_ROWS = 16
GLA_HEADS = 4
GLA_DK = 64
GLA_DV = 128
GLA_QK_W = 256
GLA_RANK = 16
GLA_TAU = 16.0
GLA_SAFE_DECAY = 60.0
TOKEN_GROUP = 16
ATT_HEADS = 8
ATT_DH = 64
ATT_WIN = 512
MAX_REL = 128
N_BRANCH = 3
D_FF = 2816
EPS = 1e-6
MASKED = -1e30
LANES = 128

MIX_U, MIX_QB, MIX_KB, MIX_VB, MIX_RB, MIX_QC, MIX_KC, MIX_VC = 0, 512, 768, 1024, 1536, 2048, 2560, 3072
MIX_W = 3584
MIXZ_W = MIX_W + LANES
GATE_W = N_BRANCH * D_MODEL
YS_A, YS_B, YS_C = 0, 512, 1024

ROW_TILE = 256
SEQ_TILE = 512
VMEM_LIMIT = 56 * 1024 * 1024

NT_DIMS = (((1,), (1,)), ((), ()))
TN_DIMS = (((0,), (0,)), ((), ()))


def _sigmoid(x):
    return 1.0 / (1.0 + jnp.exp(-x))


def _rms(x, g):
    return x * lax.rsqrt(jnp.mean(x * x, axis=-1, keepdims=True) + EPS) * g


def _layer_spec(arr, l):
    tail = (0,) * (arr.ndim - 1)
    return pl.BlockSpec((None,) + arr.shape[1:], lambda *_: (l,) + tail, pipeline_mode=pl.Buffered(1))


def _project(x_ref, g_ref, w_ref, mix_ref, z_ref):
    xb = _rms(x_ref[...], g_ref[...]).astype(BF16)
    mix_ref[...] = jnp.dot(xb, w_ref[:, :MIX_W], preferred_element_type=F32)
    z_ref[...] = jnp.dot(xb, w_ref[:, MIX_W:], preferred_element_type=F32)


def _inproj(x, pw, l):
    rows = x.shape[0]
    tm = min(ROW_TILE, rows)
    row_spec = lambda w: pl.BlockSpec((tm, w), lambda i: (i, 0))
    consts = [pw["attn_norm_g"], pw["w_mixz"]]
    return pl.pallas_call(
        _project,
        grid=(rows // tm,),
        in_specs=[row_spec(D_MODEL)] + [_layer_spec(c, l) for c in consts],
        out_specs=[row_spec(MIX_W), row_spec(LANES)],
        out_shape=[jax.ShapeDtypeStruct((rows, MIX_W), F32), jax.ShapeDtypeStruct((rows, LANES), F32)],
        compiler_params=pltpu.CompilerParams(dimension_semantics=("arbitrary",), vmem_limit_bytes=VMEM_LIMIT),
        name="in_proj",
    )(x, *consts)


def _mix_tile(tt, lg, lq, grp, has_state, pos0, tile, mix_ref, z_ref, ys_ref, consts, scratch, side_work):
    wg2_ref, bg_ref, gng_ref, pmap_ref, pscale_ref, bias_ref = consts
    carry_ref, st_ref, kbuf_ref, vbuf_ref, la_ref, stprev_ref = scratch
    lane = lax.broadcasted_iota(jnp.int32, (1, LANES), 1)
    head_mask = (lane < ATT_DH, lane >= ATT_DH)

    side_work()

    u = mix_ref[:, MIX_U:MIX_U + MIX]
    ext = jnp.concatenate([carry_ref[...], u], axis=0)
    carry_ref[...] = u[tt - HIST_ROWS:, :]
    row = lax.broadcasted_iota(jnp.int32, (tt, POOL_GW), 0)
    n_seen = (pos0 + tile * tt + row + 1).astype(F32)
    for g, w in enumerate(POOL_WINDOWS):
        cols = slice(g * POOL_GW, (g + 1) * POOL_GW)
        s = ext[:, cols]
        shift = 1
        while shift < w:
            s = s + pltpu.roll(s, shift, 0)
            shift *= 2
        p = s[HIST_ROWS:, :] / jnp.minimum(float(w), n_seen) - u[:, cols]
        y = jnp.dot(p.astype(BF16), pmap_ref[g], preferred_element_type=F32) * pscale_ref[:, cols]
        ys_ref[:, YS_A + g * POOL_GW:YS_A + (g + 1) * POOL_GW] = y.astype(BF16)

    z = z_ref[...].astype(BF16)
    pre = jnp.dot(z, wg2_ref[...], preferred_element_type=F32) + bg_ref[...]
    log_a = -(jnp.maximum(-pre, 0.0) + jnp.log1p(jnp.exp(-jnp.abs(pre)))) / GLA_TAU
    row_in_blk = lax.broadcasted_iota(jnp.int32, (tt, GLA_QK_W), 0) & (lg - 1)
    b = log_a
    shift = 1
    while shift < lg:
        b = b + jnp.where(row_in_blk >= shift, pltpu.roll(b, shift, 0), 0.0)
        shift *= 2

    ri = lax.broadcasted_iota(jnp.int32, (2 * lg, lg), 0) & (lg - 1)
    ci = lax.broadcasted_iota(jnp.int32, (2 * lg, lg), 1)
    causal = ci <= ri

    def stack_heads(x):
        return jnp.concatenate([jnp.where(head_mask[0], x, 0.0), jnp.where(head_mask[1], x, 0.0)],
                               axis=0).astype(BF16)

    def gla_out(rows, h, o):
        r = mix_ref[rows, MIX_RB + h * GLA_DV:MIX_RB + (h + 1) * GLA_DV]
        y = _rms(o, gng_ref[...]) * (r * _sigmoid(r))
        ys_ref[rows, YS_B + h * GLA_DV:YS_B + (h + 1) * GLA_DV] = y.astype(BF16)

    blocks_safe = jnp.max(-b) <= GLA_SAFE_DECAY
    stprev_ref[...] = st_ref[...]
    la_ref[...] = log_a
    for p in range(GLA_HEADS // 2):
        lanes = slice(p * LANES, (p + 1) * LANES)
        bq = b[:, lanes]
        eb = jnp.exp(bq)
        enb = jnp.exp(-bq)
        qs = mix_ref[:, MIX_QB + p * LANES:MIX_QB + (p + 1) * LANES] * (GLA_DK ** -0.5) * eb
        kt = mix_ref[:, MIX_KB + p * LANES:MIX_KB + (p + 1) * LANES] * enb
        st = st_ref[p]
        for c in range(tt // lg):
            rows = slice(c * lg, (c + 1) * lg)
            e_last = eb[(c + 1) * lg - 1:(c + 1) * lg, :]
            ktb = kt[rows].astype(BF16)
            kdb = (kt[rows] * e_last).astype(BF16)
            qst = stack_heads(qs[rows])
            att = lax.dot_general(qst, ktb, NT_DIMS, preferred_element_type=F32)
            att = jnp.where(causal, att, 0.0).astype(BF16)
            inter = lax.dot_general(qst, st.astype(BF16), NT_DIMS, preferred_element_type=F32)
            upd = []
            for hl in range(2):
                h = 2 * p + hl
                half = slice(hl * lg, (hl + 1) * lg)
                vh = mix_ref[rows, MIX_VB + h * GLA_DV:MIX_VB + (h + 1) * GLA_DV].astype(BF16)
                gla_out(rows, h, jnp.dot(att[half], vh, preferred_element_type=F32) + inter[half])
                upd.append(lax.dot_general(vh, kdb, TN_DIMS, preferred_element_type=F32))
            st = e_last * st + jnp.where(head_mask[0], upd[0], upd[1])
        st_ref[p] = st

    kbuf_ref[ATT_WIN:ATT_WIN + tt, :] = mix_ref[:, MIX_KC:MIX_KC + MIX].astype(BF16)
    vbuf_ref[ATT_WIN:ATT_WIN + tt, :] = mix_ref[:, MIX_VC:MIX_VC + MIX].astype(BF16)
    gq = grp * lq
    win = ATT_WIN + gq
    kcol = lax.broadcasted_iota(jnp.int32, (1, win), 1)
    first_tile_neg = jnp.where(tile == 0, MASKED, 0.0).astype(F32)
    for gi in range(tt // gq):
        qrows = slice(gi * gq, (gi + 1) * gq)
        krows = slice(gi * gq, gi * gq + win)
        if not has_state:
            invalid = jnp.where(kcol < ATT_WIN - gi * gq, first_tile_neg, 0.0)
        for p in range(ATT_HEADS // 2):
            cols = slice(p * LANES, (p + 1) * LANES)
            q2 = mix_ref[qrows, MIX_QC + p * LANES:MIX_QC + (p + 1) * LANES] * (ATT_DH ** -0.5)
            s = lax.dot_general(stack_heads(q2), kbuf_ref[krows, cols], NT_DIMS, preferred_element_type=F32)
            s = s + bias_ref[p]
            if not has_state:
                s = s + invalid
            e = jnp.exp(s - jnp.max(s, axis=-1, keepdims=True))
            den = jnp.sum(e, axis=-1, keepdims=True)
            o = jnp.dot(e.astype(BF16), vbuf_ref[krows, cols], preferred_element_type=F32) / den
            ys_ref[qrows, YS_C + p * LANES:YS_C + (p + 1) * LANES] = (
                jnp.where(head_mask[0], o[:gq], o[gq:]).astype(BF16))

    if not has_state:
        kbuf_ref[0:ATT_WIN, :] = kbuf_ref[tt:tt + ATT_WIN, :]
        vbuf_ref[0:ATT_WIN, :] = vbuf_ref[tt:tt + ATT_WIN, :]

    @pl.when(jnp.logical_not(blocks_safe))
    def _gla_tokens():
        eye = (lax.broadcasted_iota(jnp.int32, (LANES, LANES), 0)
               == lax.broadcasted_iota(jnp.int32, (LANES, LANES), 1)).astype(F32)
        tok = lax.broadcasted_iota(jnp.int32, (TOKEN_GROUP, LANES), 0)
        for p in range(GLA_HEADS // 2):
            def group(gidx, st, p=p):
                rows = pl.ds(pl.multiple_of(gidx * TOKEN_GROUP, TOKEN_GROUP), TOKEN_GROUP)
                a = jnp.exp(la_ref[rows, p * LANES:(p + 1) * LANES])
                q = mix_ref[rows, MIX_QB + p * LANES:MIX_QB + (p + 1) * LANES] * (GLA_DK ** -0.5)
                k = mix_ref[rows, MIX_KB + p * LANES:MIX_KB + (p + 1) * LANES]
                v_t = [lax.dot_general(eye, mix_ref[rows, MIX_VB + (2 * p + hl) * GLA_DV:
                                                    MIX_VB + (2 * p + hl + 1) * GLA_DV],
                                       NT_DIMS, precision=lax.Precision.HIGHEST, preferred_element_type=F32)
                       for hl in range(2)]
                o = [jnp.zeros((TOKEN_GROUP, GLA_DV), F32), jnp.zeros((TOKEN_GROUP, GLA_DV), F32)]
                for t in range(TOKEN_GROUP):
                    v_col = jnp.where(head_mask[0], v_t[0][:, t:t + 1], v_t[1][:, t:t + 1])
                    st = a[t:t + 1, :] * st + v_col * k[t:t + 1, :]
                    stb = st.astype(BF16)
                    for hl in range(2):
                        q_t = jnp.where(tok == t, jnp.where(head_mask[hl], q, 0.0), 0.0).astype(BF16)
                        o[hl] = o[hl] + lax.dot_general(q_t, stb, NT_DIMS, preferred_element_type=F32)
                for hl in range(2):
                    gla_out(rows, 2 * p + hl, o[hl])
                return st

            st_ref[p] = lax.fori_loop(0, tt // TOKEN_GROUP, group, stprev_ref[p])


def _write_caches(tt, mix_ref, st_ref, sfin_ref, pool_ref, knew_ref, vnew_ref):
    keep = min(ATT_WIN, tt)
    sfin_ref[0] = st_ref[...]
    pool_ref[0] = mix_ref[tt - HIST_ROWS:, MIX_U:MIX_U + MIX]
    knew_ref[0] = mix_ref[tt - keep:, MIX_KC:MIX_KC + MIX]
    vnew_ref[0] = mix_ref[tt - keep:, MIX_VC:MIX_VC + MIX]


def _mixer_sample_kernel(tt, lg, lq, grp, pos0, mix_ref, z_ref, hist_ref, s0_ref, kc_ref, vc_ref, *refs):
    consts, (ys_ref, sfin_ref, pool_ref, knew_ref, vnew_ref), scratch = refs[:6], refs[6:11], refs[11:]
    carry_ref, st_ref, kbuf_ref, vbuf_ref = scratch[:4]
    carry_ref[...] = hist_ref[0]
    st_ref[...] = s0_ref[0]
    kbuf_ref[0:ATT_WIN, :] = kc_ref[0].astype(BF16)
    vbuf_ref[0:ATT_WIN, :] = vc_ref[0].astype(BF16)
    _mix_tile(tt, lg, lq, grp, True, pos0, 0, mix_ref, z_ref, ys_ref, consts, scratch, lambda: None)
    _write_caches(tt, mix_ref, st_ref, sfin_ref, pool_ref, knew_ref, vnew_ref)


def _mixer_prompt_kernel(tt, lg, lq, grp, pos0, xf_ref, xo_ref, xn_ref, ng_ref, w_ref, *refs):
    consts, (ys_ref, sfin_ref, pool_ref, knew_ref, vnew_ref), scratch = refs[:6], refs[6:11], refs[11:]
    mix_a, z_a, mix_b, z_b = scratch[:4]
    scratch = scratch[4:]
    carry_ref, st_ref, kbuf_ref, vbuf_ref = scratch[:4]
    j = pl.program_id(1)

    @pl.when(j == 0)
    def _init():
        carry_ref[...] = jnp.zeros_like(carry_ref)
        st_ref[...] = jnp.zeros_like(st_ref)
        kbuf_ref[0:ATT_WIN, :] = jnp.zeros((ATT_WIN, MIX), BF16)
        vbuf_ref[0:ATT_WIN, :] = jnp.zeros((ATT_WIN, MIX), BF16)
        _project(xf_ref, ng_ref, w_ref, mix_a, z_a)

    mix = functools.partial(_mix_tile, tt, lg, lq, grp, False, pos0)
    mix(2 * j, mix_a, z_a, ys_ref.at[pl.ds(0, tt)], consts, scratch,
        lambda: _project(xo_ref, ng_ref, w_ref, mix_b, z_b))
    mix(2 * j + 1, mix_b, z_b, ys_ref.at[pl.ds(tt, tt)], consts, scratch,
        lambda: _project(xn_ref, ng_ref, w_ref, mix_a, z_a))

    @pl.when(j == pl.num_programs(1) - 1)
    def _fin():
        _write_caches(tt, mix_b, st_ref, sfin_ref, pool_ref, knew_ref, vnew_ref)


def _mixers(x_or_proj, state, pw, l, n_seq, seq_len, pos0):
    has_state = state is not None
    tt = min(SEQ_TILE, seq_len)
    lg = min(CHUNK, seq_len)
    lq = min(CHUNK, seq_len)
    grp = _att_group(seq_len)
    nt = seq_len // tt
    keep = min(ATT_WIN, seq_len)
    consts = [pw["w_gate2"], pw["b_gate"], pw["gla_norm_g"], pw["pool_map"], pw["pool_scale"],
              pw["bias_s"] if has_state else pw["bias_p"]]
    per_seq = lambda r: pl.BlockSpec((1, r, MIX), lambda s, i: (s, 0, 0))
    state_spec = pl.BlockSpec((1, 2, LANES, LANES), lambda s, i: (s, 0, 0, 0))
    out_shape = [jax.ShapeDtypeStruct((n_seq * seq_len, 3 * MIX), BF16),
                 jax.ShapeDtypeStruct((n_seq, 2, LANES, LANES), F32),
                 jax.ShapeDtypeStruct((n_seq, HIST_ROWS, MIX), F32),
                 jax.ShapeDtypeStruct((n_seq, keep, MIX), F32),
                 jax.ShapeDtypeStruct((n_seq, keep, MIX), F32)]
    scratch = [pltpu.VMEM((HIST_ROWS, MIX), F32),
               pltpu.VMEM((2, LANES, LANES), F32),
               pltpu.VMEM((ATT_WIN + tt, MIX), BF16),
               pltpu.VMEM((ATT_WIN + tt, MIX), BF16),
               pltpu.VMEM((tt, GLA_QK_W), F32),
               pltpu.VMEM((2, LANES, LANES), F32)]
    params = pltpu.CompilerParams(dimension_semantics=("arbitrary", "arbitrary"), vmem_limit_bytes=VMEM_LIMIT)

    if has_state:
        assert nt == 1
        mix, z = x_or_proj
        hist, s0t, kc, vc = state
        tile = lambda w: pl.BlockSpec((tt, w), lambda s, i: (s, 0))
        return pl.pallas_call(
            functools.partial(_mixer_sample_kernel, tt, lg, lq, grp, pos0),
            grid=(n_seq, 1),
            in_specs=[tile(MIX_W), tile(LANES), per_seq(HIST_ROWS), state_spec, per_seq(ATT_WIN), per_seq(ATT_WIN)]
            + [_layer_spec(c, l) for c in consts],
            out_specs=[tile(3 * MIX), state_spec, per_seq(HIST_ROWS), per_seq(keep), per_seq(keep)],
            out_shape=out_shape, scratch_shapes=scratch, compiler_params=params, name="mixers_sample",
        )(mix, z, hist, s0t, kc, vc, *consts)

    assert tt == ATT_WIN and nt % 2 == 0
    x = x_or_proj
    x_tile = lambda f, **kw: pl.BlockSpec((tt, D_MODEL), lambda s, j: (s * nt + f(j), 0), **kw)
    proj_scratch = [pltpu.VMEM((tt, MIX_W), F32), pltpu.VMEM((tt, LANES), F32),
                    pltpu.VMEM((tt, MIX_W), F32), pltpu.VMEM((tt, LANES), F32)]
    return pl.pallas_call(
        functools.partial(_mixer_prompt_kernel, tt, lg, lq, grp, pos0),
        grid=(n_seq, nt // 2),
        in_specs=[x_tile(lambda j: 0, pipeline_mode=pl.Buffered(1)),
                  x_tile(lambda j: 2 * j + 1, pipeline_mode=pl.Buffered(1)),
                  x_tile(lambda j: jnp.minimum(2 * j + 2, nt - 1), pipeline_mode=pl.Buffered(1)),
                  _layer_spec(pw["attn_norm_g"], l), _layer_spec(pw["w_mixz"], l)]
        + [_layer_spec(c, l) for c in consts],
        out_specs=[pl.BlockSpec((2 * tt, 3 * MIX), lambda s, j: (s * (nt // 2) + j, 0)),
                   state_spec, per_seq(HIST_ROWS), per_seq(keep), per_seq(keep)],
        out_shape=out_shape, scratch_shapes=proj_scratch + scratch, compiler_params=params, name="mixers_prompt",
    )(x, x, x, pw["attn_norm_g"], pw["w_mixz"], *consts)


def _merge_ffn_kernel(final, x_ref, ys_ref, ng_ref, wgate_ref, wbr_ref, wout_ref, g2_ref, wf_ref, wo_ref, fg_ref,
                      out_ref):
    x = x_ref[...]
    xb = _rms(x, ng_ref[...]).astype(BF16)
    merged = None
    for g in range(N_BRANCH):
        logits = jnp.dot(xb, wgate_ref[:, g * D_MODEL:(g + 1) * D_MODEL], preferred_element_type=F32)
        branch = jnp.dot(ys_ref[:, g * MIX:(g + 1) * MIX], wbr_ref[g], preferred_element_type=F32)
        term = _sigmoid(logits) * branch
        merged = term if merged is None else merged + term
    x1 = x + jnp.dot(merged.astype(BF16), wout_ref[...], preferred_element_type=F32)
    xn = _rms(x1, g2_ref[...]).astype(BF16)
    a = jnp.dot(xn, wf_ref[:, :D_FF], preferred_element_type=F32)
    gt = jnp.dot(xn, wf_ref[:, D_FF:], preferred_element_type=F32)
    act = (a * _sigmoid(a) * gt).astype(BF16)
    x2 = x1 + jnp.dot(act, wo_ref[...], preferred_element_type=F32)
    if final:
        x2 = _rms(x2, fg_ref[...])
    out_ref[...] = x2


def _merge_ffn(x, ys, pw, l, final):
    rows = x.shape[0]
    tm = min(ROW_TILE, rows)
    row_spec = lambda w: pl.BlockSpec((tm, w), lambda i: (i, 0))
    consts = [pw["attn_norm_g"], pw["w_gates"], pw["w_branch"], pw["w_out"], pw["ffn_norm_g"], pw["w_ffn_in"],
              pw["w_ffn_out"]]
    return pl.pallas_call(
        functools.partial(_merge_ffn_kernel, final),
        grid=(rows // tm,),
        in_specs=([row_spec(D_MODEL), row_spec(3 * MIX)] + [_layer_spec(c, l) for c in consts]
                  + [_layer_spec(pw["final_norm_g"], 0)]),
        out_specs=row_spec(D_MODEL),
        out_shape=jax.ShapeDtypeStruct((rows, D_MODEL), F32),
        compiler_params=pltpu.CompilerParams(dimension_semantics=("arbitrary",), vmem_limit_bytes=VMEM_LIMIT),
        name="merge_ffn",
    )(x, ys, *consts, pw["final_norm_g"])


def _att_group(seq_len):
    return 2 if seq_len >= 2 * CHUNK else 1


def _band_bias(rel_bias, lq, grp):
    assert lq - 1 <= MAX_REL
    depth = rel_bias.shape[0]
    rtab = rel_bias[..., ::-1].astype(F32)
    n_far = ATT_WIN + lq - MAX_REL
    ext = jnp.concatenate([jnp.broadcast_to(rtab[..., :1], (depth, ATT_HEADS, n_far)),
                           rtab[..., 1:MAX_REL + lq]], axis=-1)
    wk = ATT_WIN + lq
    period = wk + lq - 1
    rot = jnp.concatenate([ext[..., lq - 1:], ext[..., :lq - 1]], axis=-1)
    chunk = jnp.tile(rot, (1, 1, lq))[..., :lq * (period - 1)].reshape(depth, ATT_HEADS, lq, period - 1)[..., :wk]
    blocks = [jnp.pad(chunk, ((0, 0), (0, 0), (0, 0), (j * lq, (grp - 1 - j) * lq)), constant_values=MASKED)
              for j in range(grp)]
    per_head = jnp.concatenate(blocks, axis=2)
    return per_head.reshape(depth, ATT_HEADS // 2, 2 * grp * lq, ATT_WIN + grp * lq)


def _prep_params(attn_norm_g, w_in, w_gate2, b_gate, gla_norm_g, pool_map, pool_scale, rel_bias, w_branch, w_out,
                 ffn_norm_g, w_ffn_in, w_ffn_out, final_norm_g, prompt_len, sample_len):
    depth = w_in.shape[0]
    wb = w_in.astype(BF16)
    z0 = MIX_U + MIX + 2 * GLA_QK_W + 2 * MIX
    g0 = z0 + GLA_RANK + 3 * MIX
    w_mixz = jnp.concatenate([wb[:, :, :z0], wb[:, :, z0 + GLA_RANK:g0], wb[:, :, z0:z0 + GLA_RANK],
                              jnp.zeros((depth, D_MODEL, LANES - GLA_RANK), BF16)], axis=2)
    assert w_mixz.shape[2] == MIXZ_W
    wg2 = jnp.concatenate([w_gate2, jnp.zeros((depth, LANES - GLA_RANK, GLA_QK_W), w_gate2.dtype)], axis=1)
    vec = lambda a: a[:, None, :]
    return dict(
        attn_norm_g=vec(attn_norm_g), w_mixz=w_mixz, w_gates=wb[:, :, g0:], w_gate2=wg2.astype(BF16),
        b_gate=vec(b_gate), gla_norm_g=vec(gla_norm_g), pool_map=pool_map.astype(BF16), pool_scale=vec(pool_scale),
        bias_p=_band_bias(rel_bias, CHUNK, _att_group(prompt_len)),
        bias_s=_band_bias(rel_bias, min(CHUNK, sample_len), _att_group(sample_len)),
        w_branch=w_branch.astype(BF16), w_out=w_out.astype(BF16), ffn_norm_g=vec(ffn_norm_g),
        w_ffn_in=w_ffn_in.astype(BF16), w_ffn_out=w_ffn_out.astype(BF16),
        final_norm_g=final_norm_g[None, None, :])


def _layer(x, n_seq, seq_len, state, pw, l, pos0, final):
    if state is None:
        ys, s_fin_t, pool16, k_new, v_new = _mixers(x, None, pw, l, n_seq, seq_len, pos0)
    else:
        hist, s0, kc, vc = state
        hist = jnp.pad(hist, ((0, 0), (HIST_ROWS - POOL_HIST, 0), (0, 0)))
        s0t = s0.reshape(n_seq, 2, 2, GLA_DK, GLA_DV).transpose(0, 1, 4, 2, 3).reshape(n_seq, 2, LANES, LANES)
        state = (hist, s0t, kc.reshape(n_seq, ATT_WIN, MIX), vc.reshape(n_seq, ATT_WIN, MIX))
        ys, s_fin_t, pool16, k_new, v_new = _mixers(_inproj(x, pw, l), state, pw, l, n_seq, seq_len, pos0)
    x_new = _merge_ffn(x, ys, pw, l, final)

    keep = min(ATT_WIN, seq_len)
    gla_new = (s_fin_t.reshape(n_seq, 2, GLA_DV, 2, GLA_DK).transpose(0, 1, 3, 4, 2)
               .reshape(n_seq, GLA_HEADS, GLA_DK, GLA_DV))
    return x_new, (pool16[:, HIST_ROWS - POOL_HIST:], gla_new,
                   k_new.reshape(n_seq, keep, ATT_HEADS, ATT_DH), v_new.reshape(n_seq, keep, ATT_HEADS, ATT_DH))


def kernel(x_prompt, x_sample, cache_pool, state_gla, cache_k, cache_v, attn_norm_g, w_in, w_gate2, b_gate,
           gla_norm_g, pool_map, pool_scale, rel_bias, w_branch, w_out, ffn_norm_g, w_ffn_in, w_ffn_out,
           final_norm_g):
    batch, seq, _ = x_prompt.shape
    dec_batch, dec_seq, _ = x_sample.shape
    depth = w_in.shape[0]
    hp = x_prompt.reshape(batch * seq, D_MODEL)
    hs = x_sample.reshape(dec_batch * dec_seq, D_MODEL)
    pw = _prep_params(attn_norm_g, w_in, w_gate2, b_gate, gla_norm_g, pool_map, pool_scale, rel_bias, w_branch,
                      w_out, ffn_norm_g, w_ffn_in, w_ffn_out, final_norm_g, seq, dec_seq)
    outs_p, outs_s = [], []
    for l in range(depth):
        final = l == depth - 1
        hp, sp = _layer(hp, batch, seq, None, pw, l, 0, final)
        hs, ss = _layer(hs, dec_batch, dec_seq, (cache_pool[l], state_gla[l], cache_k[l], cache_v[l]), pw, l,
                        PAST_LEN, final)
        outs_p.append(sp)
        outs_s.append(ss)
    stack = lambda outs, j: jnp.stack([o[j] for o in outs])
    return (hp.reshape(batch, seq, D_MODEL), hs.reshape(dec_batch, dec_seq, D_MODEL),
            stack(outs_p, 0), stack(outs_s, 0), stack(outs_p, 1), stack(outs_s, 1),
            stack(outs_p, 2), stack(outs_s, 2), stack(outs_p, 3), stack(outs_s, 3))
```

```python
import functools

import jax
import jax.numpy as jnp
from jax import lax
from jax.experimental import pallas as pl
from jax.experimental.pallas import tpu as pltpu

F32 = jnp.float32
BF16 = jnp.bfloat16

D_MODEL = 1024
PAST_LEN = 2048
CHUNK = 64
MIX = 512
POOL_WINDOWS = (2, 4, 8, 16)
POOL_GW = 128
POOL_HIST = 15
HIST_ROWS = 16
GLA_HEADS = 4
GLA_DK = 64
GLA_DV = 128
GLA_QK_W = 256
GLA_RANK = 16
GLA_TAU = 16.0
GLA_SAFE_DECAY = 60.0
TOKEN_GROUP = 16
ATT_HEADS = 8
ATT_DH = 64
ATT_WIN = 512
MAX_REL = 128
N_BRANCH = 3
D_FF = 2816
EPS = 1e-6
MASKED = -1e30
LANES = 128

MIX_U, MIX_QB, MIX_KB, MIX_VB, MIX_RB, MIX_QC, MIX_KC, MIX_VC = 0, 512, 768, 1024, 1536, 2048, 2560, 3072
MIX_W = 3584
GATE_W = N_BRANCH * D_MODEL
W_MIX, W_GATES, W_Z = 0, MIX_W, MIX_W + GATE_W
W_IN_COLS = W_Z + LANES
YS_A, YS_B, YS_C = 0, 512, 1024

ROW_TILE = 256
SEQ_TILE = 512
VMEM_LIMIT = 56 * 1024 * 1024

NT_DIMS = (((1,), (1,)), ((), ()))
TN_DIMS = (((0,), (0,)), ((), ()))


def _sigmoid(x):
    return 1.0 / (1.0 + jnp.exp(-x))


def _rms(x, g):
    return x * lax.rsqrt(jnp.mean(x * x, axis=-1, keepdims=True) + EPS) * g


def _layer_spec(arr, l):
    tail = (0,) * (arr.ndim - 1)
    return pl.BlockSpec((None,) + arr.shape[1:], lambda *_: (l,) + tail, pipeline_mode=pl.Buffered(1))


def _inproj_kernel(x_ref, g_ref, w_ref, om_ref, og_ref, oz_ref):
    xb = _rms(x_ref[...], g_ref[...]).astype(BF16)
    om_ref[...] = jnp.dot(xb, w_ref[:, W_MIX:W_MIX + MIX_W], preferred_element_type=F32)
    og_ref[...] = jnp.dot(xb, w_ref[:, W_GATES:W_GATES + GATE_W], preferred_element_type=F32)
    oz_ref[...] = jnp.dot(xb, w_ref[:, W_Z:W_Z + LANES], preferred_element_type=F32)


def _inproj(x, pw, l):
    rows = x.shape[0]
    tm = min(ROW_TILE, rows)
    row_spec = lambda w: pl.BlockSpec((tm, w), lambda i: (i, 0))
    consts = [pw["attn_norm_g"], pw["w_in"]]
    return pl.pallas_call(
        _inproj_kernel,
        grid=(rows // tm,),
        in_specs=[row_spec(D_MODEL)] + [_layer_spec(c, l) for c in consts],
        out_specs=[row_spec(MIX_W), row_spec(GATE_W), row_spec(LANES)],
        out_shape=[jax.ShapeDtypeStruct((rows, MIX_W), F32), jax.ShapeDtypeStruct((rows, GATE_W), F32),
                   jax.ShapeDtypeStruct((rows, LANES), F32)],
        compiler_params=pltpu.CompilerParams(dimension_semantics=("arbitrary",), vmem_limit_bytes=VMEM_LIMIT),
        name="in_proj",
    )(x, *consts)


def _mixer_kernel(tt, lg, lq, grp, has_state, pos0, *refs):
    refs = list(refs)
    mix_ref, z_ref = refs[:2]
    k = 2
    if has_state:
        hist_ref, s0_ref, kc_ref, vc_ref = refs[k:k + 4]
        k += 4
    wg2_ref, bg_ref, gng_ref, pmap_ref, pscale_ref, bias_ref = refs[k:k + 6]
    k += 6
    ys_ref, sfin_ref, pool_ref, knew_ref, vnew_ref = refs[k:k + 5]
    carry_ref, st_ref, kbuf_ref, vbuf_ref, la_ref = refs[k + 5:]

    i = pl.program_id(1)
    lane = lax.broadcasted_iota(jnp.int32, (1, LANES), 1)
    head_mask = (lane < ATT_DH, lane >= ATT_DH)

    @pl.when(i == 0)
    def _init():
        if has_state:
            carry_ref[...] = hist_ref[0]
            st_ref[...] = s0_ref[0]
            kbuf_ref[0:ATT_WIN, :] = kc_ref[0].astype(BF16)
            vbuf_ref[0:ATT_WIN, :] = vc_ref[0].astype(BF16)
        else:
            carry_ref[...] = jnp.zeros_like(carry_ref)
            st_ref[...] = jnp.zeros_like(st_ref)
            kbuf_ref[0:ATT_WIN, :] = jnp.zeros((ATT_WIN, MIX), BF16)
            vbuf_ref[0:ATT_WIN, :] = jnp.zeros((ATT_WIN, MIX), BF16)

    u = mix_ref[:, MIX_U:MIX_U + MIX]
    ext = jnp.concatenate([carry_ref[...], u], axis=0)
    carry_ref[...] = u[tt - HIST_ROWS:, :]
    row = lax.broadcasted_iota(jnp.int32, (tt, POOL_GW), 0)
    n_seen = (pos0 + i * tt + row + 1).astype(F32)
    for g, w in enumerate(POOL_WINDOWS):
        cols = slice(g * POOL_GW, (g + 1) * POOL_GW)
        s = ext[:, cols]
        shift = 1
        while shift < w:
            s = s + pltpu.roll(s, shift, 0)
            shift *= 2
        p = s[HIST_ROWS:, :] / jnp.minimum(float(w), n_seen) - u[:, cols]
        y = jnp.dot(p.astype(BF16), pmap_ref[g], preferred_element_type=F32) * pscale_ref[:, cols]
        ys_ref[:, YS_A + g * POOL_GW:YS_A + (g + 1) * POOL_GW] = y.astype(BF16)

    z = z_ref[...].astype(BF16)
    pre = jnp.dot(z, wg2_ref[...], preferred_element_type=F32) + bg_ref[...]
    log_a = -(jnp.maximum(-pre, 0.0) + jnp.log1p(jnp.exp(-jnp.abs(pre)))) / GLA_TAU
    row_in_blk = lax.broadcasted_iota(jnp.int32, (tt, GLA_QK_W), 0) & (lg - 1)
    b = log_a
    shift = 1
    while shift < lg:
        b = b + jnp.where(row_in_blk >= shift, pltpu.roll(b, shift, 0), 0.0)
        shift *= 2

    ri = lax.broadcasted_iota(jnp.int32, (2 * lg, lg), 0) & (lg - 1)
    ci = lax.broadcasted_iota(jnp.int32, (2 * lg, lg), 1)
    causal = ci <= ri

    def stack_heads(x):
        return jnp.concatenate([jnp.where(head_mask[0], x, 0.0), jnp.where(head_mask[1], x, 0.0)],
                               axis=0).astype(BF16)

    def gla_out(rows, h, o):
        r = mix_ref[rows, MIX_RB + h * GLA_DV:MIX_RB + (h + 1) * GLA_DV]
        y = _rms(o, gng_ref[...]) * (r * _sigmoid(r))
        ys_ref[rows, YS_B + h * GLA_DV:YS_B + (h + 1) * GLA_DV] = y.astype(BF16)

    blocks_safe = jnp.max(-b) <= GLA_SAFE_DECAY

    @pl.when(blocks_safe)
    def _gla_blocks():
        for p in range(GLA_HEADS // 2):
            lanes = slice(p * LANES, (p + 1) * LANES)
            bq = b[:, lanes]
            eb = jnp.exp(bq)
            enb = jnp.exp(-bq)
            qs = mix_ref[:, MIX_QB + p * LANES:MIX_QB + (p + 1) * LANES] * (GLA_DK ** -0.5) * eb
            kt = mix_ref[:, MIX_KB + p * LANES:MIX_KB + (p + 1) * LANES] * enb
            st = st_ref[p]
            for c in range(tt // lg):
                rows = slice(c * lg, (c + 1) * lg)
                e_last = eb[(c + 1) * lg - 1:(c + 1) * lg, :]
                ktb = kt[rows].astype(BF16)
                kdb = (kt[rows] * e_last).astype(BF16)
                qst = stack_heads(qs[rows])
                att = lax.dot_general(qst, ktb, NT_DIMS, preferred_element_type=F32)
                att = jnp.where(causal, att, 0.0).astype(BF16)
                inter = lax.dot_general(qst, st.astype(BF16), NT_DIMS, preferred_element_type=F32)
                upd = []
                for hl in range(2):
                    h = 2 * p + hl
                    half = slice(hl * lg, (hl + 1) * lg)
                    vh = mix_ref[rows, MIX_VB + h * GLA_DV:MIX_VB + (h + 1) * GLA_DV].astype(BF16)
                    gla_out(rows, h, jnp.dot(att[half], vh, preferred_element_type=F32) + inter[half])
                    upd.append(lax.dot_general(vh, kdb, TN_DIMS, preferred_element_type=F32))
                st = e_last * st + jnp.where(head_mask[0], upd[0], upd[1])
            st_ref[p] = st

    @pl.when(jnp.logical_not(blocks_safe))
    def _gla_tokens():
        la_ref[...] = log_a
        eye = (lax.broadcasted_iota(jnp.int32, (LANES, LANES), 0)
               == lax.broadcasted_iota(jnp.int32, (LANES, LANES), 1)).astype(F32)
        tok = lax.broadcasted_iota(jnp.int32, (TOKEN_GROUP, LANES), 0)
        for p in range(GLA_HEADS // 2):
            def group(gidx, st, p=p):
                rows = pl.ds(pl.multiple_of(gidx * TOKEN_GROUP, TOKEN_GROUP), TOKEN_GROUP)
                a = jnp.exp(la_ref[rows, p * LANES:(p + 1) * LANES])
                q = mix_ref[rows, MIX_QB + p * LANES:MIX_QB + (p + 1) * LANES] * (GLA_DK ** -0.5)
                k = mix_ref[rows, MIX_KB + p * LANES:MIX_KB + (p + 1) * LANES]
                v_t = [lax.dot_general(eye, mix_ref[rows, MIX_VB + (2 * p + hl) * GLA_DV:
                                                    MIX_VB + (2 * p + hl + 1) * GLA_DV],
                                       NT_DIMS, precision=lax.Precision.HIGHEST, preferred_element_type=F32)
                       for hl in range(2)]
                o = [jnp.zeros((TOKEN_GROUP, GLA_DV), F32), jnp.zeros((TOKEN_GROUP, GLA_DV), F32)]
                for t in range(TOKEN_GROUP):
                    v_col = jnp.where(head_mask[0], v_t[0][:, t:t + 1], v_t[1][:, t:t + 1])
                    st = a[t:t + 1, :] * st + v_col * k[t:t + 1, :]
                    stb = st.astype(BF16)
                    for hl in range(2):
                        q_t = jnp.where(tok == t, jnp.where(head_mask[hl], q, 0.0), 0.0).astype(BF16)
                        o[hl] = o[hl] + lax.dot_general(q_t, stb, NT_DIMS, preferred_element_type=F32)
                for hl in range(2):
                    gla_out(rows, 2 * p + hl, o[hl])
                return st

            st_ref[p] = lax.fori_loop(0, tt // TOKEN_GROUP, group, st_ref[p])

    kbuf_ref[ATT_WIN:ATT_WIN + tt, :] = mix_ref[:, MIX_KC:MIX_KC + MIX].astype(BF16)
    vbuf_ref[ATT_WIN:ATT_WIN + tt, :] = mix_ref[:, MIX_VC:MIX_VC + MIX].astype(BF16)
    gq = grp * lq
    win = ATT_WIN + gq
    kcol = lax.broadcasted_iota(jnp.int32, (1, win), 1)
    first_tile_neg = jnp.where(i == 0, MASKED, 0.0).astype(F32)
    for gi in range(tt // gq):
        qrows = slice(gi * gq, (gi + 1) * gq)
        krows = slice(gi * gq, gi * gq + win)
        if not has_state:
            invalid = jnp.where(kcol < ATT_WIN - gi * gq, first_tile_neg, 0.0)
        for p in range(ATT_HEADS // 2):
            cols = slice(p * LANES, (p + 1) * LANES)
            q2 = mix_ref[qrows, MIX_QC + p * LANES:MIX_QC + (p + 1) * LANES] * (ATT_DH ** -0.5)
            s = lax.dot_general(stack_heads(q2), kbuf_ref[krows, cols], NT_DIMS, preferred_element_type=F32)
            s = s + bias_ref[p]
            if not has_state:
                s = s + invalid
            e = jnp.exp(s - jnp.max(s, axis=-1, keepdims=True))
            den = jnp.sum(e, axis=-1, keepdims=True)
            o = jnp.dot(e.astype(BF16), vbuf_ref[krows, cols], preferred_element_type=F32) / den
            ys_ref[qrows, YS_C + p * LANES:YS_C + (p + 1) * LANES] = (
                jnp.where(head_mask[0], o[:gq], o[gq:]).astype(BF16))

    if not has_state:
        kbuf_ref[0:ATT_WIN, :] = kbuf_ref[tt:tt + ATT_WIN, :]
        vbuf_ref[0:ATT_WIN, :] = vbuf_ref[tt:tt + ATT_WIN, :]

    @pl.when(i == pl.num_programs(1) - 1)
    def _fin():
        keep = min(ATT_WIN, tt)
        sfin_ref[0] = st_ref[...]
        pool_ref[0] = mix_ref[tt - HIST_ROWS:, MIX_U:MIX_U + MIX]
        knew_ref[0] = mix_ref[tt - keep:, MIX_KC:MIX_KC + MIX]
        vnew_ref[0] = mix_ref[tt - keep:, MIX_VC:MIX_VC + MIX]


def _mixers(mix, z, state, pw, l, n_seq, seq_len, pos0):
    has_state = state is not None
    tt = min(SEQ_TILE, seq_len)
    lg = min(CHUNK, seq_len)
    lq = min(CHUNK, seq_len)
    grp = _att_group(seq_len)
    nt = seq_len // tt
    assert has_state or tt == ATT_WIN
    kernel = functools.partial(_mixer_kernel, tt, lg, lq, grp, has_state, pos0)

    tile = lambda w: pl.BlockSpec((tt, w), lambda s, i: (s * nt + i, 0))
    in_specs = [tile(MIX_W), tile(LANES)]
    args = [mix, z]
    if has_state:
        hist, s0t, kc, vc = state
        in_specs += [pl.BlockSpec((1, HIST_ROWS, MIX), lambda s, i: (s, 0, 0)),
                     pl.BlockSpec((1, 2, LANES, LANES), lambda s, i: (s, 0, 0, 0)),
                     pl.BlockSpec((1, ATT_WIN, MIX), lambda s, i: (s, 0, 0)),
                     pl.BlockSpec((1, ATT_WIN, MIX), lambda s, i: (s, 0, 0))]
        args += [hist, s0t, kc, vc]
    consts = [pw["w_gate2"], pw["b_gate"], pw["gla_norm_g"], pw["pool_map"], pw["pool_scale"],
              pw["bias_s"] if has_state else pw["bias_p"]]
    in_specs += [_layer_spec(c, l) for c in consts]
    args += consts
    keep = min(ATT_WIN, seq_len)
    per_seq = lambda r: pl.BlockSpec((1, r, MIX), lambda s, i: (s, 0, 0))

    return pl.pallas_call(
        kernel,
        grid=(n_seq, nt),
        in_specs=in_specs,
        out_specs=[pl.BlockSpec((tt, 3 * MIX), lambda s, i: (s * nt + i, 0)),
                   pl.BlockSpec((1, 2, LANES, LANES), lambda s, i: (s, 0, 0, 0)),
                   per_seq(HIST_ROWS), per_seq(keep), per_seq(keep)],
        out_shape=[jax.ShapeDtypeStruct((n_seq * seq_len, 3 * MIX), BF16),
                   jax.ShapeDtypeStruct((n_seq, 2, LANES, LANES), F32),
                   jax.ShapeDtypeStruct((n_seq, HIST_ROWS, MIX), F32),
                   jax.ShapeDtypeStruct((n_seq, keep, MIX), F32),
                   jax.ShapeDtypeStruct((n_seq, keep, MIX), F32)],
        scratch_shapes=[pltpu.VMEM((HIST_ROWS, MIX), F32),
                        pltpu.VMEM((2, LANES, LANES), F32),
                        pltpu.VMEM((ATT_WIN + tt, MIX), BF16),
                        pltpu.VMEM((ATT_WIN + tt, MIX), BF16),
                        pltpu.VMEM((tt, GLA_QK_W), F32)],
        compiler_params=pltpu.CompilerParams(dimension_semantics=("arbitrary", "arbitrary"),
                                             vmem_limit_bytes=VMEM_LIMIT),
        name="mixers_sample" if has_state else "mixers_prompt",
    )(*args)


def _merge_ffn_kernel(final, x_ref, ys_ref, hg_ref, wbr_ref, wout_ref, g2_ref, wf_ref, wo_ref, fg_ref, out_ref):
    merged = None
    for g in range(N_BRANCH):
        branch = jnp.dot(ys_ref[:, g * MIX:(g + 1) * MIX], wbr_ref[g], preferred_element_type=F32)
        term = _sigmoid(hg_ref[:, g * D_MODEL:(g + 1) * D_MODEL]) * branch
        merged = term if merged is None else merged + term
    x1 = x_ref[...] + jnp.dot(merged.astype(BF16), wout_ref[...], preferred_element_type=F32)
    xn = _rms(x1, g2_ref[...]).astype(BF16)
    a = jnp.dot(xn, wf_ref[:, :D_FF], preferred_element_type=F32)
    gt = jnp.dot(xn, wf_ref[:, D_FF:], preferred_element_type=F32)
    act = (a * _sigmoid(a) * gt).astype(BF16)
    x2 = x1 + jnp.dot(act, wo_ref[...], preferred_element_type=F32)
    if final:
        x2 = _rms(x2, fg_ref[...])
    out_ref[...] = x2


def _merge_ffn(x, ys, gates, pw, l, final):
    rows = x.shape[0]
    tm = min(ROW_TILE, rows)
    row_spec = lambda w: pl.BlockSpec((tm, w), lambda i: (i, 0))
    consts = [pw["w_branch"], pw["w_out"], pw["ffn_norm_g"], pw["w_ffn_in"], pw["w_ffn_out"]]
    return pl.pallas_call(
        functools.partial(_merge_ffn_kernel, final),
        grid=(rows // tm,),
        in_specs=([row_spec(D_MODEL), row_spec(3 * MIX), row_spec(GATE_W)] + [_layer_spec(c, l) for c in consts]
                  + [_layer_spec(pw["final_norm_g"], 0)]),
        out_specs=row_spec(D_MODEL),
        out_shape=jax.ShapeDtypeStruct((rows, D_MODEL), F32),
        compiler_params=pltpu.CompilerParams(dimension_semantics=("arbitrary",), vmem_limit_bytes=VMEM_LIMIT),
        name="merge_ffn",
    )(x, ys, gates, *consts, pw["final_norm_g"])


def _att_group(seq_len):
    return 2 if seq_len >= 2 * CHUNK else 1


def _band_bias(rel_bias, lq, grp):
    assert lq - 1 <= MAX_REL
    depth = rel_bias.shape[0]
    rtab = rel_bias[..., ::-1].astype(F32)
    n_far = ATT_WIN + lq - MAX_REL
    ext = jnp.concatenate([jnp.broadcast_to(rtab[..., :1], (depth, ATT_HEADS, n_far)),
                           rtab[..., 1:MAX_REL + lq]], axis=-1)
    wk = ATT_WIN + lq
    period = wk + lq - 1
    rot = jnp.concatenate([ext[..., lq - 1:], ext[..., :lq - 1]], axis=-1)
    chunk = jnp.tile(rot, (1, 1, lq))[..., :lq * (period - 1)].reshape(depth, ATT_HEADS, lq, period - 1)[..., :wk]
    blocks = [jnp.pad(chunk, ((0, 0), (0, 0), (0, 0), (j * lq, (grp - 1 - j) * lq)), constant_values=MASKED)
              for j in range(grp)]
    per_head = jnp.concatenate(blocks, axis=2)
    return per_head.reshape(depth, ATT_HEADS // 2, 2 * grp * lq, ATT_WIN + grp * lq)


def _prep_params(attn_norm_g, w_in, w_gate2, b_gate, gla_norm_g, pool_map, pool_scale, rel_bias, w_branch, w_out,
                 ffn_norm_g, w_ffn_in, w_ffn_out, final_norm_g, prompt_len, sample_len):
    depth = w_in.shape[0]
    wb = w_in.astype(BF16)
    z0 = MIX_U + MIX + 2 * GLA_QK_W + 2 * MIX
    w_in_b = jnp.concatenate([wb[:, :, :z0], wb[:, :, z0 + GLA_RANK:], wb[:, :, z0:z0 + GLA_RANK],
                              jnp.zeros((depth, D_MODEL, LANES - GLA_RANK), BF16)], axis=2)
    assert w_in_b.shape[2] == W_IN_COLS
    wg2 = jnp.concatenate([w_gate2, jnp.zeros((depth, LANES - GLA_RANK, GLA_QK_W), w_gate2.dtype)], axis=1)
    vec = lambda a: a[:, None, :]
    return dict(
        attn_norm_g=vec(attn_norm_g), w_in=w_in_b, w_gate2=wg2.astype(BF16), b_gate=vec(b_gate),
        gla_norm_g=vec(gla_norm_g), pool_map=pool_map.astype(BF16), pool_scale=vec(pool_scale),
        bias_p=_band_bias(rel_bias, CHUNK, _att_group(prompt_len)),
        bias_s=_band_bias(rel_bias, min(CHUNK, sample_len), _att_group(sample_len)),
        w_branch=w_branch.astype(BF16), w_out=w_out.astype(BF16), ffn_norm_g=vec(ffn_norm_g),
        w_ffn_in=w_ffn_in.astype(BF16), w_ffn_out=w_ffn_out.astype(BF16),
        final_norm_g=final_norm_g[None, None, :])


def _layer(x, n_seq, seq_len, state, pw, l, pos0, final):
    mix, gates, z = _inproj(x, pw, l)
    if state is not None:
        hist, s0, kc, vc = state
        hist = jnp.pad(hist, ((0, 0), (HIST_ROWS - POOL_HIST, 0), (0, 0)))
        s0t = s0.reshape(n_seq, 2, 2, GLA_DK, GLA_DV).transpose(0, 1, 4, 2, 3).reshape(n_seq, 2, LANES, LANES)
        state = (hist, s0t, kc.reshape(n_seq, ATT_WIN, MIX), vc.reshape(n_seq, ATT_WIN, MIX))
    ys, s_fin_t, pool16, k_new, v_new = _mixers(mix, z, state, pw, l, n_seq, seq_len, pos0)
    x_new = _merge_ffn(x, ys, gates, pw, l, final)

    keep = min(ATT_WIN, seq_len)
    gla_new = (s_fin_t.reshape(n_seq, 2, GLA_DV, 2, GLA_DK).transpose(0, 1, 3, 4, 2)
               .reshape(n_seq, GLA_HEADS, GLA_DK, GLA_DV))
    return x_new, (pool16[:, HIST_ROWS - POOL_HIST:], gla_new,
                   k_new.reshape(n_seq, keep, ATT_HEADS, ATT_DH), v_new.reshape(n_seq, keep, ATT_HEADS, ATT_DH))


def kernel(x_prompt, x_sample, cache_pool, state_gla, cache_k, cache_v, attn_norm_g, w_in, w_gate2, b_gate,
           gla_norm_g, pool_map, pool_scale, rel_bias, w_branch, w_out, ffn_norm_g, w_ffn_in, w_ffn_out,
           final_norm_g):
    batch, seq, _ = x_prompt.shape
    dec_batch, dec_seq, _ = x_sample.shape
    depth = w_in.shape[0]
    hp = x_prompt.reshape(batch * seq, D_MODEL)
    hs = x_sample.reshape(dec_batch * dec_seq, D_MODEL)
    pw = _prep_params(attn_norm_g, w_in, w_gate2, b_gate, gla_norm_g, pool_map, pool_scale, rel_bias, w_branch,
                      w_out, ffn_norm_g, w_ffn_in, w_ffn_out, final_norm_g, seq, dec_seq)
    outs_p, outs_s = [], []
    for l in range(depth):
        final = l == depth - 1
        hp, sp = _layer(hp, batch, seq, None, pw, l, 0, final)
        hs, ss = _layer(hs, dec_batch, dec_seq, (cache_pool[l], state_gla[l], cache_k[l], cache_v[l]), pw, l,
                        PAST_LEN, final)
        outs_p.append(sp)
        outs_s.append(ss)
    stack = lambda outs, j: jnp.stack([o[j] for o in outs])
    return (hp.reshape(batch, seq, D_MODEL), hs.reshape(dec_batch, dec_seq, D_MODEL),
            stack(outs_p, 0), stack(outs_s, 0), stack(outs_p, 1), stack(outs_s, 1),
            stack(outs_p, 2), stack(outs_s, 2), stack(outs_p, 3), stack(outs_s, 3))
```

```python
import functools

import jax
import jax.numpy as jnp
from jax import lax
from jax.experimental import pallas as pl
from jax.experimental.pallas import tpu as pltpu

F32 = jnp.float32
BF16 = jnp.bfloat16

D_MODEL = 1024
PAST_LEN = 2048
CHUNK = 64
MIX = 512
POOL_WINDOWS = (2, 4, 8, 16)
POOL_GW = 128
POOL_HIST = 15
HIST_ROWS = 16
GLA_HEADS = 4
GLA_DK = 64
GLA_DV = 128
GLA_QK_W = 256
GLA_RANK = 16
GLA_TAU = 16.0
GLA_SAFE_DECAY = 60.0
TOKEN_GROUP = 16
ATT_HEADS = 8
ATT_DH = 64
ATT_WIN = 512
MAX_REL = 128
N_BRANCH = 3
D_FF = 2816
EPS = 1e-6
MASKED = -1e30
LANES = 128

MIX_U, MIX_VB, MIX_RB, MIX_QC, MIX_KC, MIX_VC = 0, 512, 1024, 1536, 2048, 2560
MIX_W = 3072
GLA_Q, GLA_K, GLA_Z = 0, 256, 512
GLA_W = 640
GATE_W = N_BRANCH * D_MODEL
YS_A, YS_B, YS_C = 0, 512, 1024

ROW_TILE = 256
SEQ_TILE = 512
VMEM_LIMIT = 56 * 1024 * 1024

NT_DIMS = (((1,), (1,)), ((), ()))
TN_DIMS = (((0,), (0,)), ((), ()))


def _sigmoid(x):
    return 1.0 / (1.0 + jnp.exp(-x))


def _rms(x, g):
    return x * lax.rsqrt(jnp.mean(x * x, axis=-1, keepdims=True) + EPS) * g


def _layer_spec(arr, l):
    tail = (0,) * (arr.ndim - 1)
    return pl.BlockSpec((None,) + arr.shape[1:], lambda *_: (l,) + tail, pipeline_mode=pl.Buffered(1))


def _inproj_kernel(x_ref, g_ref, wg_ref, wm_ref, wl_ref, og_ref, om_ref, ol_ref):
    xb = _rms(x_ref[...], g_ref[...]).astype(BF16)
    og_ref[...] = jnp.dot(xb, wg_ref[...], preferred_element_type=F32)
    om_ref[...] = jnp.dot(xb, wm_ref[...], preferred_element_type=F32)
    ol_ref[...] = jnp.dot(xb, wl_ref[...], preferred_element_type=F32)


def _inproj(x, pw, l):
    rows = x.shape[0]
    tm = min(ROW_TILE, rows)
    row_spec = lambda w: pl.BlockSpec((tm, w), lambda i: (i, 0))
    consts = [pw["attn_norm_g"], pw["w_gates"], pw["w_mix"], pw["w_gla"]]
    return pl.pallas_call(
        _inproj_kernel,
        grid=(rows // tm,),
        in_specs=[row_spec(D_MODEL)] + [_layer_spec(c, l) for c in consts],
        out_specs=[row_spec(GATE_W), row_spec(MIX_W), row_spec(GLA_W)],
        out_shape=[jax.ShapeDtypeStruct((rows, GATE_W), F32), jax.ShapeDtypeStruct((rows, MIX_W), F32),
                   jax.ShapeDtypeStruct((rows, GLA_W), F32)],
        compiler_params=pltpu.CompilerParams(dimension_semantics=("arbitrary",), vmem_limit_bytes=VMEM_LIMIT),
        name="in_proj",
    )(x, *consts)


def _mixer_kernel(tt, lg, lq, grp, has_state, pos0, *refs):
    refs = list(refs)
    mix_ref, gla_ref = refs[:2]
    k = 2
    if has_state:
        hist_ref, s0_ref, kc_ref, vc_ref = refs[k:k + 4]
        k += 4
    wg2_ref, bg_ref, gng_ref, pmap_ref, pscale_ref, bias_ref = refs[k:k + 6]
    k += 6
    ys_ref, sfin_ref, pool_ref, knew_ref, vnew_ref = refs[k:k + 5]
    carry_ref, st_ref, kbuf_ref, vbuf_ref, la_ref = refs[k + 5:]

    i = pl.program_id(1)
    lane = lax.broadcasted_iota(jnp.int32, (1, LANES), 1)
    head_mask = (lane < ATT_DH, lane >= ATT_DH)

    @pl.when(i == 0)
    def _init():
        if has_state:
            carry_ref[...] = hist_ref[0]
            st_ref[...] = s0_ref[0]
            kbuf_ref[0:ATT_WIN, :] = kc_ref[0].astype(BF16)
            vbuf_ref[0:ATT_WIN, :] = vc_ref[0].astype(BF16)
        else:
            carry_ref[...] = jnp.zeros_like(carry_ref)
            st_ref[...] = jnp.zeros_like(st_ref)
            kbuf_ref[0:ATT_WIN, :] = jnp.zeros((ATT_WIN, MIX), BF16)
            vbuf_ref[0:ATT_WIN, :] = jnp.zeros((ATT_WIN, MIX), BF16)

    u = mix_ref[:, MIX_U:MIX_U + MIX]
    ext = jnp.concatenate([carry_ref[...], u], axis=0)
    carry_ref[...] = u[tt - HIST_ROWS:, :]
    row = lax.broadcasted_iota(jnp.int32, (tt, POOL_GW), 0)
    n_seen = (pos0 + i * tt + row + 1).astype(F32)
    for g, w in enumerate(POOL_WINDOWS):
        cols = slice(g * POOL_GW, (g + 1) * POOL_GW)
        s = ext[:, cols]
        shift = 1
        while shift < w:
            s = s + pltpu.roll(s, shift, 0)
            shift *= 2
        p = s[HIST_ROWS:, :] / jnp.minimum(float(w), n_seen) - u[:, cols]
        y = jnp.dot(p.astype(BF16), pmap_ref[g], preferred_element_type=F32) * pscale_ref[:, cols]
        ys_ref[:, YS_A + g * POOL_GW:YS_A + (g + 1) * POOL_GW] = y.astype(BF16)

    z = gla_ref[:, GLA_Z:GLA_Z + LANES].astype(BF16)
    pre = jnp.dot(z, wg2_ref[...], preferred_element_type=F32) + bg_ref[...]
    log_a = -(jnp.maximum(-pre, 0.0) + jnp.log1p(jnp.exp(-jnp.abs(pre)))) / GLA_TAU
    row_in_blk = lax.broadcasted_iota(jnp.int32, (tt, GLA_QK_W), 0) & (lg - 1)
    b = log_a
    shift = 1
    while shift < lg:
        b = b + jnp.where(row_in_blk >= shift, pltpu.roll(b, shift, 0), 0.0)
        shift *= 2

    ri = lax.broadcasted_iota(jnp.int32, (2 * lg, lg), 0) & (lg - 1)
    ci = lax.broadcasted_iota(jnp.int32, (2 * lg, lg), 1)
    causal = ci <= ri

    def stack_heads(x):
        return jnp.concatenate([jnp.where(head_mask[0], x, 0.0), jnp.where(head_mask[1], x, 0.0)],
                               axis=0).astype(BF16)

    def gla_out(rows, h, o):
        r = mix_ref[rows, MIX_RB + h * GLA_DV:MIX_RB + (h + 1) * GLA_DV]
        y = _rms(o, gng_ref[...]) * (r * _sigmoid(r))
        ys_ref[rows, YS_B + h * GLA_DV:YS_B + (h + 1) * GLA_DV] = y.astype(BF16)

    blocks_safe = jnp.max(-b) <= GLA_SAFE_DECAY

    @pl.when(blocks_safe)
    def _gla_blocks():
        for p in range(GLA_HEADS // 2):
            lanes = slice(p * LANES, (p + 1) * LANES)
            bq = b[:, lanes]
            eb = jnp.exp(bq)
            enb = jnp.exp(-bq)
            qs = gla_ref[:, GLA_Q + p * LANES:GLA_Q + (p + 1) * LANES] * (GLA_DK ** -0.5) * eb
            kt = gla_ref[:, GLA_K + p * LANES:GLA_K + (p + 1) * LANES] * enb
            n_blk = tt // lg
            blk = lambda c: slice(c * lg, (c + 1) * lg)
            vb = [[mix_ref[blk(c), MIX_VB + (2 * p + hl) * GLA_DV:MIX_VB + (2 * p + hl + 1) * GLA_DV].astype(BF16)
                   for hl in range(2)] for c in range(n_blk)]
            e_last = [eb[(c + 1) * lg - 1:(c + 1) * lg, :] for c in range(n_blk)]
            qst = [stack_heads(qs[blk(c)]) for c in range(n_blk)]
            att, upd = [], []
            for c in range(n_blk):
                a = lax.dot_general(qst[c], kt[blk(c)].astype(BF16), NT_DIMS, preferred_element_type=F32)
                att.append(jnp.where(causal, a, 0.0).astype(BF16))
                kdb = (kt[blk(c)] * e_last[c]).astype(BF16)
                u2 = [lax.dot_general(vb[c][hl], kdb, TN_DIMS, preferred_element_type=F32) for hl in range(2)]
                upd.append(jnp.where(head_mask[0], u2[0], u2[1]))
            starts = [st_ref[p]]
            for c in range(n_blk):
                starts.append(e_last[c] * starts[c] + upd[c])
            st_ref[p] = starts[n_blk]
            for c in range(n_blk):
                inter = lax.dot_general(qst[c], starts[c].astype(BF16), NT_DIMS, preferred_element_type=F32)
                for hl in range(2):
                    half = slice(hl * lg, (hl + 1) * lg)
                    gla_out(blk(c), 2 * p + hl,
                            jnp.dot(att[c][half], vb[c][hl], preferred_element_type=F32) + inter[half])

    @pl.when(jnp.logical_not(blocks_safe))
    def _gla_tokens():
        la_ref[...] = log_a
        eye = (lax.broadcasted_iota(jnp.int32, (LANES, LANES), 0)
               == lax.broadcasted_iota(jnp.int32, (LANES, LANES), 1)).astype(F32)
        tok = lax.broadcasted_iota(jnp.int32, (TOKEN_GROUP, LANES), 0)
        for p in range(GLA_HEADS // 2):
            def group(gidx, st, p=p):
                rows = pl.ds(pl.multiple_of(gidx * TOKEN_GROUP, TOKEN_GROUP), TOKEN_GROUP)
                a = jnp.exp(la_ref[rows, p * LANES:(p + 1) * LANES])
                q = gla_ref[rows, GLA_Q + p * LANES:GLA_Q + (p + 1) * LANES] * (GLA_DK ** -0.5)
                k = gla_ref[rows, GLA_K + p * LANES:GLA_K + (p + 1) * LANES]
                v_t = [lax.dot_general(eye, mix_ref[rows, MIX_VB + (2 * p + hl) * GLA_DV:
                                                    MIX_VB + (2 * p + hl + 1) * GLA_DV],
                                       NT_DIMS, precision=lax.Precision.HIGHEST, preferred_element_type=F32)
                       for hl in range(2)]
                o = [jnp.zeros((TOKEN_GROUP, GLA_DV), F32), jnp.zeros((TOKEN_GROUP, GLA_DV), F32)]
                for t in range(TOKEN_GROUP):
                    v_col = jnp.where(head_mask[0], v_t[0][:, t:t + 1], v_t[1][:, t:t + 1])
                    st = a[t:t + 1, :] * st + v_col * k[t:t + 1, :]
                    stb = st.astype(BF16)
                    for hl in range(2):
                        q_t = jnp.where(tok == t, jnp.where(head_mask[hl], q, 0.0), 0.0).astype(BF16)
                        o[hl] = o[hl] + lax.dot_general(q_t, stb, NT_DIMS, preferred_element_type=F32)
                for hl in range(2):
                    gla_out(rows, 2 * p + hl, o[hl])
                return st

            st_ref[p] = lax.fori_loop(0, tt // TOKEN_GROUP, group, st_ref[p])

    kbuf_ref[ATT_WIN:ATT_WIN + tt, :] = mix_ref[:, MIX_KC:MIX_KC + MIX].astype(BF16)
    vbuf_ref[ATT_WIN:ATT_WIN + tt, :] = mix_ref[:, MIX_VC:MIX_VC + MIX].astype(BF16)
    gq = grp * lq
    win = ATT_WIN + gq
    kcol = lax.broadcasted_iota(jnp.int32, (1, win), 1)
    first_tile_neg = jnp.where(i == 0, MASKED, 0.0).astype(F32)
    for gi in range(tt // gq):
        qrows = slice(gi * gq, (gi + 1) * gq)
        krows = slice(gi * gq, gi * gq + win)
        if not has_state:
            invalid = jnp.where(kcol < ATT_WIN - gi * gq, first_tile_neg, 0.0)
        for p in range(ATT_HEADS // 2):
            cols = slice(p * LANES, (p + 1) * LANES)
            q2 = mix_ref[qrows, MIX_QC + p * LANES:MIX_QC + (p + 1) * LANES] * (ATT_DH ** -0.5)
            s = lax.dot_general(stack_heads(q2), kbuf_ref[krows, cols], NT_DIMS, preferred_element_type=F32)
            s = s + bias_ref[p]
            if not has_state:
                s = s + invalid
            e = jnp.exp(s - jnp.max(s, axis=-1, keepdims=True))
            den = jnp.sum(e, axis=-1, keepdims=True)
            o = jnp.dot(e.astype(BF16), vbuf_ref[krows, cols], preferred_element_type=F32) / den
            ys_ref[qrows, YS_C + p * LANES:YS_C + (p + 1) * LANES] = (
                jnp.where(head_mask[0], o[:gq], o[gq:]).astype(BF16))

    if not has_state:
        kbuf_ref[0:ATT_WIN, :] = kbuf_ref[tt:tt + ATT_WIN, :]
        vbuf_ref[0:ATT_WIN, :] = vbuf_ref[tt:tt + ATT_WIN, :]

    @pl.when(i == pl.num_programs(1) - 1)
    def _fin():
        keep = min(ATT_WIN, tt)
        sfin_ref[0] = st_ref[...]
        pool_ref[0] = mix_ref[tt - HIST_ROWS:, MIX_U:MIX_U + MIX]
        knew_ref[0] = mix_ref[tt - keep:, MIX_KC:MIX_KC + MIX]
        vnew_ref[0] = mix_ref[tt - keep:, MIX_VC:MIX_VC + MIX]


def _mixers(mix, gla, state, pw, l, n_seq, seq_len, pos0):
    has_state = state is not None
    tt = min(SEQ_TILE, seq_len)
    lg = min(CHUNK, seq_len)
    lq = min(CHUNK, seq_len)
    grp = _att_group(seq_len)
    nt = seq_len // tt
    assert has_state or tt == ATT_WIN
    kernel = functools.partial(_mixer_kernel, tt, lg, lq, grp, has_state, pos0)

    tile = lambda w: pl.BlockSpec((tt, w), lambda s, i: (s * nt + i, 0))
    in_specs = [tile(MIX_W), tile(GLA_W)]
    args = [mix, gla]
    if has_state:
        hist, s0t, kc, vc = state
        in_specs += [pl.BlockSpec((1, HIST_ROWS, MIX), lambda s, i: (s, 0, 0)),
                     pl.BlockSpec((1, 2, LANES, LANES), lambda s, i: (s, 0, 0, 0)),
                     pl.BlockSpec((1, ATT_WIN, MIX), lambda s, i: (s, 0, 0)),
                     pl.BlockSpec((1, ATT_WIN, MIX), lambda s, i: (s, 0, 0))]
        args += [hist, s0t, kc, vc]
    consts = [pw["w_gate2"], pw["b_gate"], pw["gla_norm_g"], pw["pool_map"], pw["pool_scale"],
              pw["bias_s"] if has_state else pw["bias_p"]]
    in_specs += [_layer_spec(c, l) for c in consts]
    args += consts
    keep = min(ATT_WIN, seq_len)
    per_seq = lambda r: pl.BlockSpec((1, r, MIX), lambda s, i: (s, 0, 0))

    return pl.pallas_call(
        kernel,
        grid=(n_seq, nt),
        in_specs=in_specs,
        out_specs=[pl.BlockSpec((tt, 3 * MIX), lambda s, i: (s * nt + i, 0)),
                   pl.BlockSpec((1, 2, LANES, LANES), lambda s, i: (s, 0, 0, 0)),
                   per_seq(HIST_ROWS), per_seq(keep), per_seq(keep)],
        out_shape=[jax.ShapeDtypeStruct((n_seq * seq_len, 3 * MIX), BF16),
                   jax.ShapeDtypeStruct((n_seq, 2, LANES, LANES), F32),
                   jax.ShapeDtypeStruct((n_seq, HIST_ROWS, MIX), F32),
                   jax.ShapeDtypeStruct((n_seq, keep, MIX), F32),
                   jax.ShapeDtypeStruct((n_seq, keep, MIX), F32)],
        scratch_shapes=[pltpu.VMEM((HIST_ROWS, MIX), F32),
                        pltpu.VMEM((2, LANES, LANES), F32),
                        pltpu.VMEM((ATT_WIN + tt, MIX), BF16),
                        pltpu.VMEM((ATT_WIN + tt, MIX), BF16),
                        pltpu.VMEM((tt, GLA_QK_W), F32)],
        compiler_params=pltpu.CompilerParams(dimension_semantics=("arbitrary", "arbitrary"),
                                             vmem_limit_bytes=VMEM_LIMIT),
        name="mixers_sample" if has_state else "mixers_prompt",
    )(*args)


def _merge_ffn_kernel(final, x_ref, ys_ref, hg_ref, wbr_ref, wout_ref, g2_ref, wf_ref, wo_ref, fg_ref, out_ref):
    merged = None
    for g in range(N_BRANCH):
        branch = jnp.dot(ys_ref[:, g * MIX:(g + 1) * MIX], wbr_ref[g], preferred_element_type=F32)
        term = _sigmoid(hg_ref[:, g * D_MODEL:(g + 1) * D_MODEL]) * branch
        merged = term if merged is None else merged + term
    x1 = x_ref[...] + jnp.dot(merged.astype(BF16), wout_ref[...], preferred_element_type=F32)
    xn = _rms(x1, g2_ref[...]).astype(BF16)
    a = jnp.dot(xn, wf_ref[:, :D_FF], preferred_element_type=F32)
    gt = jnp.dot(xn, wf_ref[:, D_FF:], preferred_element_type=F32)
    act = (a * _sigmoid(a) * gt).astype(BF16)
    x2 = x1 + jnp.dot(act, wo_ref[...], preferred_element_type=F32)
    if final:
        x2 = _rms(x2, fg_ref[...])
    out_ref[...] = x2


def _merge_ffn(x, ys, gates, pw, l, final):
    rows = x.shape[0]
    tm = min(ROW_TILE, rows)
    row_spec = lambda w: pl.BlockSpec((tm, w), lambda i: (i, 0))
    consts = [pw["w_branch"], pw["w_out"], pw["ffn_norm_g"], pw["w_ffn_in"], pw["w_ffn_out"]]
    return pl.pallas_call(
        functools.partial(_merge_ffn_kernel, final),
        grid=(rows // tm,),
        in_specs=([row_spec(D_MODEL), row_spec(3 * MIX), row_spec(GATE_W)] + [_layer_spec(c, l) for c in consts]
                  + [_layer_spec(pw["final_norm_g"], 0)]),
        out_specs=row_spec(D_MODEL),
        out_shape=jax.ShapeDtypeStruct((rows, D_MODEL), F32),
        compiler_params=pltpu.CompilerParams(dimension_semantics=("arbitrary",), vmem_limit_bytes=VMEM_LIMIT),
        name="merge_ffn",
    )(x, ys, gates, *consts, pw["final_norm_g"])


def _att_group(seq_len):
    return 2 if seq_len >= 2 * CHUNK else 1


def _band_bias(rel_bias, lq, grp):
    assert lq - 1 <= MAX_REL
    depth = rel_bias.shape[0]
    rtab = rel_bias[..., ::-1].astype(F32)
    n_far = ATT_WIN + lq - MAX_REL
    ext = jnp.concatenate([jnp.broadcast_to(rtab[..., :1], (depth, ATT_HEADS, n_far)),
                           rtab[..., 1:MAX_REL + lq]], axis=-1)
    wk = ATT_WIN + lq
    period = wk + lq - 1
    rot = jnp.concatenate([ext[..., lq - 1:], ext[..., :lq - 1]], axis=-1)
    chunk = jnp.tile(rot, (1, 1, lq))[..., :lq * (period - 1)].reshape(depth, ATT_HEADS, lq, period - 1)[..., :wk]
    blocks = [jnp.pad(chunk, ((0, 0), (0, 0), (0, 0), (j * lq, (grp - 1 - j) * lq)), constant_values=MASKED)
              for j in range(grp)]
    per_head = jnp.concatenate(blocks, axis=2)
    return per_head.reshape(depth, ATT_HEADS // 2, 2 * grp * lq, ATT_WIN + grp * lq)


def _prep_params(attn_norm_g, w_in, w_gate2, b_gate, gla_norm_g, pool_map, pool_scale, rel_bias, w_branch, w_out,
                 ffn_norm_g, w_ffn_in, w_ffn_out, final_norm_g, prompt_len, sample_len):
    depth = w_in.shape[0]
    o = [0, 512, 768, 1024, 1536, 2048, 2064, 2576, 3088, 3600, 6672]
    col = lambda a, b: w_in[:, :, o[a]:o[b]]
    w_mix = jnp.concatenate([col(0, 1), col(3, 4), col(4, 5), col(6, 7), col(7, 8), col(8, 9)], axis=2)
    w_gla = jnp.concatenate([col(1, 2), col(2, 3), col(5, 6),
                             jnp.zeros((depth, D_MODEL, LANES - GLA_RANK), w_in.dtype)], axis=2)
    wg2 = jnp.concatenate([w_gate2, jnp.zeros((depth, LANES - GLA_RANK, GLA_QK_W), w_gate2.dtype)], axis=1)
    vec = lambda a: a[:, None, :]
    return dict(
        attn_norm_g=vec(attn_norm_g), w_gates=col(9, 10).astype(BF16), w_mix=w_mix.astype(BF16),
        w_gla=w_gla.astype(BF16), w_gate2=wg2.astype(BF16), b_gate=vec(b_gate),
        gla_norm_g=vec(gla_norm_g), pool_map=pool_map.astype(BF16), pool_scale=vec(pool_scale),
        bias_p=_band_bias(rel_bias, CHUNK, _att_group(prompt_len)),
        bias_s=_band_bias(rel_bias, min(CHUNK, sample_len), _att_group(sample_len)),
        w_branch=w_branch.astype(BF16), w_out=w_out.astype(BF16), ffn_norm_g=vec(ffn_norm_g),
        w_ffn_in=w_ffn_in.astype(BF16), w_ffn_out=w_ffn_out.astype(BF16),
        final_norm_g=final_norm_g[None, None, :])


def _layer(x, n_seq, seq_len, state, pw, l, pos0, final):
    gates, mix, gla = _inproj(x, pw, l)
    if state is not None:
        hist, s0, kc, vc = state
        hist = jnp.pad(hist, ((0, 0), (HIST_ROWS - POOL_HIST, 0), (0, 0)))
        s0t = s0.reshape(n_seq, 2, 2, GLA_DK, GLA_DV).transpose(0, 1, 4, 2, 3).reshape(n_seq, 2, LANES, LANES)
        state = (hist, s0t, kc.reshape(n_seq, ATT_WIN, MIX), vc.reshape(n_seq, ATT_WIN, MIX))
    ys, s_fin_t, pool16, k_new, v_new = _mixers(mix, gla, state, pw, l, n_seq, seq_len, pos0)
    x_new = _merge_ffn(x, ys, gates, pw, l, final)

    keep = min(ATT_WIN, seq_len)
    gla_new = (s_fin_t.reshape(n_seq, 2, GLA_DV, 2, GLA_DK).transpose(0, 1, 3, 4, 2)
               .reshape(n_seq, GLA_HEADS, GLA_DK, GLA_DV))
    return x_new, (pool16[:, HIST_ROWS - POOL_HIST:], gla_new,
                   k_new.reshape(n_seq, keep, ATT_HEADS, ATT_DH), v_new.reshape(n_seq, keep, ATT_HEADS, ATT_DH))


def kernel(x_prompt, x_sample, cache_pool, state_gla, cache_k, cache_v, attn_norm_g, w_in, w_gate2, b_gate,
           gla_norm_g, pool_map, pool_scale, rel_bias, w_branch, w_out, ffn_norm_g, w_ffn_in, w_ffn_out,
           final_norm_g):
    batch, seq, _ = x_prompt.shape
    dec_batch, dec_seq, _ = x_sample.shape
    depth = w_in.shape[0]
    hp = x_prompt.reshape(batch * seq, D_MODEL)
    hs = x_sample.reshape(dec_batch * dec_seq, D_MODEL)
    pw = _prep_params(attn_norm_g, w_in, w_gate2, b_gate, gla_norm_g, pool_map, pool_scale, rel_bias, w_branch,
                      w_out, ffn_norm_g, w_ffn_in, w_ffn_out, final_norm_g, seq, dec_seq)
    outs_p, outs_s = [], []
    for l in range(depth):
        final = l == depth - 1
        hp, sp = _layer(hp, batch, seq, None, pw, l, 0, final)
        hs, ss = _layer(hs, dec_batch, dec_seq, (cache_pool[l], state_gla[l], cache_k[l], cache_v[l]), pw, l,
                        PAST_LEN, final)
        outs_p.append(sp)
        outs_s.append(ss)
    stack = lambda outs, j: jnp.stack([o[j] for o in outs])
    return (hp.reshape(batch, seq, D_MODEL), hs.reshape(dec_batch, dec_seq, D_MODEL),
            stack(outs_p, 0), stack(outs_s, 0), stack(outs_p, 1), stack(outs_s, 1),
            stack(outs_p, 2), stack(outs_s, 2), stack(outs_p, 3), stack(outs_s, 3))
```

```python
import functools

import jax
import jax.numpy as jnp
from jax import lax
from jax.experimental import pallas as pl
from jax.experimental.pallas import tpu as pltpu

F32 = jnp.float32
BF16 = jnp.bfloat16

D_MODEL = 1024
PAST_LEN = 2048
CHUNK = 64
MIX = 512
POOL_WINDOWS = (2, 4, 8, 16)
POOL_GW = 128
POOL_HIST = 15
HIST_ROWS = 16
GLA_HEADS = 4
GLA_DK = 64
GLA_DV = 128
GLA_QK_W = 256
GLA_RANK = 16
GLA_TAU = 16.0
GLA_SAFE_DECAY = 60.0
TOKEN_GROUP = 16
ATT_HEADS = 8
ATT_DH = 64
ATT_WIN = 512
MAX_REL = 128
N_BRANCH = 3
D_FF = 2816
EPS = 1e-6
MASKED = -1e30
LANES = 128

MIX_U, MIX_VB, MIX_RB, MIX_QC, MIX_KC, MIX_VC = 0, 512, 1024, 1536, 2048, 2560
MIX_W = 3072
GLA_Q, GLA_K, GLA_Z = 0, 256, 512
GLA_W = 640
GATE_W = N_BRANCH * D_MODEL
YS_A, YS_B, YS_C = 0, 512, 1024

ATT_STAGE_GAP = 1
ROW_TILE = 256
SEQ_TILE = 512
VMEM_LIMIT = 56 * 1024 * 1024

NT_DIMS = (((1,), (1,)), ((), ()))
TN_DIMS = (((0,), (0,)), ((), ()))


def _sigmoid(x):
    return 1.0 / (1.0 + jnp.exp(-x))


def _rms(x, g):
    return x * lax.rsqrt(jnp.mean(x * x, axis=-1, keepdims=True) + EPS) * g


def _layer_spec(arr, l):
    tail = (0,) * (arr.ndim - 1)
    return pl.BlockSpec((None,) + arr.shape[1:], lambda *_: (l,) + tail, pipeline_mode=pl.Buffered(1))


def _inproj_kernel(x_ref, g_ref, wg_ref, wm_ref, wl_ref, og_ref, om_ref, ol_ref):
    xb = _rms(x_ref[...], g_ref[...]).astype(BF16)
    og_ref[...] = jnp.dot(xb, wg_ref[...], preferred_element_type=F32)
    om_ref[...] = jnp.dot(xb, wm_ref[...], preferred_element_type=F32)
    ol_ref[...] = jnp.dot(xb, wl_ref[...], preferred_element_type=F32)


def _inproj(x, pw, l):
    rows = x.shape[0]
    tm = min(ROW_TILE, rows)
    row_spec = lambda w: pl.BlockSpec((tm, w), lambda i: (i, 0))
    consts = [pw["attn_norm_g"], pw["w_gates"], pw["w_mix"], pw["w_gla"]]
    return pl.pallas_call(
        _inproj_kernel,
        grid=(rows // tm,),
        in_specs=[row_spec(D_MODEL)] + [_layer_spec(c, l) for c in consts],
        out_specs=[row_spec(GATE_W), row_spec(MIX_W), row_spec(GLA_W)],
        out_shape=[jax.ShapeDtypeStruct((rows, GATE_W), F32), jax.ShapeDtypeStruct((rows, MIX_W), F32),
                   jax.ShapeDtypeStruct((rows, GLA_W), F32)],
        compiler_params=pltpu.CompilerParams(dimension_semantics=("arbitrary",), vmem_limit_bytes=VMEM_LIMIT),
        name="in_proj",
    )(x, *consts)


def _mixer_kernel(tt, lg, lq, grp, has_state, pos0, *refs):
    refs = list(refs)
    mix_ref, gla_ref = refs[:2]
    k = 2
    if has_state:
        hist_ref, s0_ref, kc_ref, vc_ref = refs[k:k + 4]
        k += 4
    wg2_ref, bg_ref, gng_ref, pmap_ref, pscale_ref, bias_ref = refs[k:k + 6]
    k += 6
    ys_ref, sfin_ref, pool_ref, knew_ref, vnew_ref = refs[k:k + 5]
    carry_ref, st_ref, kbuf_ref, vbuf_ref, la_ref = refs[k + 5:]

    i = pl.program_id(1)
    lane = lax.broadcasted_iota(jnp.int32, (1, LANES), 1)
    head_mask = (lane < ATT_DH, lane >= ATT_DH)

    @pl.when(i == 0)
    def _init():
        if has_state:
            carry_ref[...] = hist_ref[0]
            st_ref[...] = s0_ref[0]
            kbuf_ref[0:ATT_WIN, :] = kc_ref[0].astype(BF16)
            vbuf_ref[0:ATT_WIN, :] = vc_ref[0].astype(BF16)
        else:
            carry_ref[...] = jnp.zeros_like(carry_ref)
            st_ref[...] = jnp.zeros_like(st_ref)
            kbuf_ref[0:ATT_WIN, :] = jnp.zeros((ATT_WIN, MIX), BF16)
            vbuf_ref[0:ATT_WIN, :] = jnp.zeros((ATT_WIN, MIX), BF16)

    u = mix_ref[:, MIX_U:MIX_U + MIX]
    ext = jnp.concatenate([carry_ref[...], u], axis=0)
    carry_ref[...] = u[tt - HIST_ROWS:, :]
    row = lax.broadcasted_iota(jnp.int32, (tt, POOL_GW), 0)
    n_seen = (pos0 + i * tt + row + 1).astype(F32)
    for g, w in enumerate(POOL_WINDOWS):
        cols = slice(g * POOL_GW, (g + 1) * POOL_GW)
        s = ext[:, cols]
        shift = 1
        while shift < w:
            s = s + pltpu.roll(s, shift, 0)
            shift *= 2
        p = s[HIST_ROWS:, :] / jnp.minimum(float(w), n_seen) - u[:, cols]
        y = jnp.dot(p.astype(BF16), pmap_ref[g], preferred_element_type=F32) * pscale_ref[:, cols]
        ys_ref[:, YS_A + g * POOL_GW:YS_A + (g + 1) * POOL_GW] = y.astype(BF16)

    z = gla_ref[:, GLA_Z:GLA_Z + LANES].astype(BF16)
    pre = jnp.dot(z, wg2_ref[...], preferred_element_type=F32) + bg_ref[...]
    log_a = -(jnp.maximum(-pre, 0.0) + jnp.log1p(jnp.exp(-jnp.abs(pre)))) / GLA_TAU
    row_in_blk = lax.broadcasted_iota(jnp.int32, (tt, GLA_QK_W), 0) & (lg - 1)
    b = log_a
    shift = 1
    while shift < lg:
        b = b + jnp.where(row_in_blk >= shift, pltpu.roll(b, shift, 0), 0.0)
        shift *= 2

    ri = lax.broadcasted_iota(jnp.int32, (2 * lg, lg), 0) & (lg - 1)
    ci = lax.broadcasted_iota(jnp.int32, (2 * lg, lg), 1)
    causal = ci <= ri

    def stack_heads(x):
        return jnp.concatenate([jnp.where(head_mask[0], x, 0.0), jnp.where(head_mask[1], x, 0.0)],
                               axis=0).astype(BF16)

    def gla_out(rows, h, o):
        r = mix_ref[rows, MIX_RB + h * GLA_DV:MIX_RB + (h + 1) * GLA_DV]
        y = _rms(o, gng_ref[...]) * (r * _sigmoid(r))
        ys_ref[rows, YS_B + h * GLA_DV:YS_B + (h + 1) * GLA_DV] = y.astype(BF16)

    blocks_safe = jnp.max(-b) <= GLA_SAFE_DECAY

    @pl.when(blocks_safe)
    def _gla_blocks():
        for p in range(GLA_HEADS // 2):
            lanes = slice(p * LANES, (p + 1) * LANES)
            bq = b[:, lanes]
            eb = jnp.exp(bq)
            enb = jnp.exp(-bq)
            qs = gla_ref[:, GLA_Q + p * LANES:GLA_Q + (p + 1) * LANES] * (GLA_DK ** -0.5) * eb
            kt = gla_ref[:, GLA_K + p * LANES:GLA_K + (p + 1) * LANES] * enb
            n_blk = tt // lg
            blk = lambda c: slice(c * lg, (c + 1) * lg)
            vb = [[mix_ref[blk(c), MIX_VB + (2 * p + hl) * GLA_DV:MIX_VB + (2 * p + hl + 1) * GLA_DV].astype(BF16)
                   for hl in range(2)] for c in range(n_blk)]
            e_last = [eb[(c + 1) * lg - 1:(c + 1) * lg, :] for c in range(n_blk)]
            qst = [stack_heads(qs[blk(c)]) for c in range(n_blk)]
            att, upd = [], []
            for c in range(n_blk):
                a = lax.dot_general(qst[c], kt[blk(c)].astype(BF16), NT_DIMS, preferred_element_type=F32)
                att.append(jnp.where(causal, a, 0.0).astype(BF16))
                kdb = (kt[blk(c)] * e_last[c]).astype(BF16)
                u2 = [lax.dot_general(vb[c][hl], kdb, TN_DIMS, preferred_element_type=F32) for hl in range(2)]
                upd.append(jnp.where(head_mask[0], u2[0], u2[1]))
            starts = [st_ref[p]]
            for c in range(n_blk):
                starts.append(e_last[c] * starts[c] + upd[c])
            st_ref[p] = starts[n_blk]
            for c in range(n_blk):
                inter = lax.dot_general(qst[c], starts[c].astype(BF16), NT_DIMS, preferred_element_type=F32)
                for hl in range(2):
                    half = slice(hl * lg, (hl + 1) * lg)
                    gla_out(blk(c), 2 * p + hl,
                            jnp.dot(att[c][half], vb[c][hl], preferred_element_type=F32) + inter[half])

    @pl.when(jnp.logical_not(blocks_safe))
    def _gla_tokens():
        la_ref[...] = log_a
        eye = (lax.broadcasted_iota(jnp.int32, (LANES, LANES), 0)
               == lax.broadcasted_iota(jnp.int32, (LANES, LANES), 1)).astype(F32)
        tok = lax.broadcasted_iota(jnp.int32, (TOKEN_GROUP, LANES), 0)
        for p in range(GLA_HEADS // 2):
            def group(gidx, st, p=p):
                rows = pl.ds(pl.multiple_of(gidx * TOKEN_GROUP, TOKEN_GROUP), TOKEN_GROUP)
                a = jnp.exp(la_ref[rows, p * LANES:(p + 1) * LANES])
                q = gla_ref[rows, GLA_Q + p * LANES:GLA_Q + (p + 1) * LANES] * (GLA_DK ** -0.5)
                k = gla_ref[rows, GLA_K + p * LANES:GLA_K + (p + 1) * LANES]
                v_t = [lax.dot_general(eye, mix_ref[rows, MIX_VB + (2 * p + hl) * GLA_DV:
                                                    MIX_VB + (2 * p + hl + 1) * GLA_DV],
                                       NT_DIMS, precision=lax.Precision.HIGHEST, preferred_element_type=F32)
                       for hl in range(2)]
                o = [jnp.zeros((TOKEN_GROUP, GLA_DV), F32), jnp.zeros((TOKEN_GROUP, GLA_DV), F32)]
                for t in range(TOKEN_GROUP):
                    v_col = jnp.where(head_mask[0], v_t[0][:, t:t + 1], v_t[1][:, t:t + 1])
                    st = a[t:t + 1, :] * st + v_col * k[t:t + 1, :]
                    stb = st.astype(BF16)
                    for hl in range(2):
                        q_t = jnp.where(tok == t, jnp.where(head_mask[hl], q, 0.0), 0.0).astype(BF16)
                        o[hl] = o[hl] + lax.dot_general(q_t, stb, NT_DIMS, preferred_element_type=F32)
                for hl in range(2):
                    gla_out(rows, 2 * p + hl, o[hl])
                return st

            st_ref[p] = lax.fori_loop(0, tt // TOKEN_GROUP, group, st_ref[p])

    kbuf_ref[ATT_WIN:ATT_WIN + tt, :] = mix_ref[:, MIX_KC:MIX_KC + MIX].astype(BF16)
    vbuf_ref[ATT_WIN:ATT_WIN + tt, :] = mix_ref[:, MIX_VC:MIX_VC + MIX].astype(BF16)
    gq = grp * lq
    win = ATT_WIN + gq
    kcol = lax.broadcasted_iota(jnp.int32, (1, win), 1)
    first_tile_neg = jnp.where(i == 0, MASKED, 0.0).astype(F32)

    def scores(gi, p):
        q2 = mix_ref[gi * gq:(gi + 1) * gq, MIX_QC + p * LANES:MIX_QC + (p + 1) * LANES] * (ATT_DH ** -0.5)
        s = lax.dot_general(stack_heads(q2), kbuf_ref[gi * gq:gi * gq + win, p * LANES:(p + 1) * LANES],
                            NT_DIMS, preferred_element_type=F32)
        s = s + bias_ref[p]
        if not has_state:
            s = s + jnp.where(kcol < ATT_WIN - gi * gq, first_tile_neg, 0.0)
        return s

    def weights(s):
        e = jnp.exp(s - jnp.max(s, axis=-1, keepdims=True))
        return e.astype(BF16), jnp.sum(e, axis=-1, keepdims=True)

    def attend(gi, p, e, den):
        o = jnp.dot(e, vbuf_ref[gi * gq:gi * gq + win, p * LANES:(p + 1) * LANES],
                    preferred_element_type=F32) / den
        ys_ref[gi * gq:(gi + 1) * gq, YS_C + p * LANES:YS_C + (p + 1) * LANES] = (
            jnp.where(head_mask[0], o[:gq], o[gq:]).astype(BF16))

    units = [(gi, p) for gi in range(tt // gq) for p in range(ATT_HEADS // 2)]
    s_q, w_q = {}, {}
    for t in range(len(units) + 2 * ATT_STAGE_GAP):
        if t < len(units):
            s_q[t] = scores(*units[t])
        if ATT_STAGE_GAP <= t < len(units) + ATT_STAGE_GAP:
            w_q[t - ATT_STAGE_GAP] = weights(s_q.pop(t - ATT_STAGE_GAP))
        if t >= 2 * ATT_STAGE_GAP:
            attend(*units[t - 2 * ATT_STAGE_GAP], *w_q.pop(t - 2 * ATT_STAGE_GAP))

    if not has_state:
        kbuf_ref[0:ATT_WIN, :] = kbuf_ref[tt:tt + ATT_WIN, :]
        vbuf_ref[0:ATT_WIN, :] = vbuf_ref[tt:tt + ATT_WIN, :]

    @pl.when(i == pl.num_programs(1) - 1)
    def _fin():
        keep = min(ATT_WIN, tt)
        sfin_ref[0] = st_ref[...]
        pool_ref[0] = mix_ref[tt - HIST_ROWS:, MIX_U:MIX_U + MIX]
        knew_ref[0] = mix_ref[tt - keep:, MIX_KC:MIX_KC + MIX]
        vnew_ref[0] = mix_ref[tt - keep:, MIX_VC:MIX_VC + MIX]


def _mixers(mix, gla, state, pw, l, n_seq, seq_len, pos0):
    has_state = state is not None
    tt = min(SEQ_TILE, seq_len)
    lg = min(CHUNK, seq_len)
    lq = min(CHUNK, seq_len)
    grp = _att_group(seq_len)
    nt = seq_len // tt
    assert has_state or tt == ATT_WIN
    kernel = functools.partial(_mixer_kernel, tt, lg, lq, grp, has_state, pos0)

    tile = lambda w: pl.BlockSpec((tt, w), lambda s, i: (s * nt + i, 0))
    in_specs = [tile(MIX_W), tile(GLA_W)]
    args = [mix, gla]
    if has_state:
        hist, s0t, kc, vc = state
        in_specs += [pl.BlockSpec((1, HIST_ROWS, MIX), lambda s, i: (s, 0, 0)),
                     pl.BlockSpec((1, 2, LANES, LANES), lambda s, i: (s, 0, 0, 0)),
                     pl.BlockSpec((1, ATT_WIN, MIX), lambda s, i: (s, 0, 0)),
                     pl.BlockSpec((1, ATT_WIN, MIX), lambda s, i: (s, 0, 0))]
        args += [hist, s0t, kc, vc]
    consts = [pw["w_gate2"], pw["b_gate"], pw["gla_norm_g"], pw["pool_map"], pw["pool_scale"],
              pw["bias_s"] if has_state else pw["bias_p"]]
    in_specs += [_layer_spec(c, l) for c in consts]
    args += consts
    keep = min(ATT_WIN, seq_len)
    per_seq = lambda r: pl.BlockSpec((1, r, MIX), lambda s, i: (s, 0, 0))

    return pl.pallas_call(
        kernel,
        grid=(n_seq, nt),
        in_specs=in_specs,
        out_specs=[pl.BlockSpec((tt, 3 * MIX), lambda s, i: (s * nt + i, 0)),
                   pl.BlockSpec((1, 2, LANES, LANES), lambda s, i: (s, 0, 0, 0)),
                   per_seq(HIST_ROWS), per_seq(keep), per_seq(keep)],
        out_shape=[jax.ShapeDtypeStruct((n_seq * seq_len, 3 * MIX), BF16),
                   jax.ShapeDtypeStruct((n_seq, 2, LANES, LANES), F32),
                   jax.ShapeDtypeStruct((n_seq, HIST_ROWS, MIX), F32),
                   jax.ShapeDtypeStruct((n_seq, keep, MIX), F32),
                   jax.ShapeDtypeStruct((n_seq, keep, MIX), F32)],
        scratch_shapes=[pltpu.VMEM((HIST_ROWS, MIX), F32),
                        pltpu.VMEM((2, LANES, LANES), F32),
                        pltpu.VMEM((ATT_WIN + tt, MIX), BF16),
                        pltpu.VMEM((ATT_WIN + tt, MIX), BF16),
                        pltpu.VMEM((tt, GLA_QK_W), F32)],
        compiler_params=pltpu.CompilerParams(dimension_semantics=("arbitrary", "arbitrary"),
                                             vmem_limit_bytes=VMEM_LIMIT),
        name="mixers_sample" if has_state else "mixers_prompt",
    )(*args)


def _merge_ffn_kernel(final, x_ref, ys_ref, hg_ref, wbr_ref, wout_ref, g2_ref, wf_ref, wo_ref, fg_ref, out_ref):
    merged = None
    for g in range(N_BRANCH):
        branch = jnp.dot(ys_ref[:, g * MIX:(g + 1) * MIX], wbr_ref[g], preferred_element_type=F32)
        term = _sigmoid(hg_ref[:, g * D_MODEL:(g + 1) * D_MODEL]) * branch
        merged = term if merged is None else merged + term
    x1 = x_ref[...] + jnp.dot(merged.astype(BF16), wout_ref[...], preferred_element_type=F32)
    xn = _rms(x1, g2_ref[...]).astype(BF16)
    a = jnp.dot(xn, wf_ref[:, :D_FF], preferred_element_type=F32)
    gt = jnp.dot(xn, wf_ref[:, D_FF:], preferred_element_type=F32)
    act = (a * _sigmoid(a) * gt).astype(BF16)
    x2 = x1 + jnp.dot(act, wo_ref[...], preferred_element_type=F32)
    if final:
        x2 = _rms(x2, fg_ref[...])
    out_ref[...] = x2


def _merge_ffn(x, ys, gates, pw, l, final):
    rows = x.shape[0]
    tm = min(ROW_TILE, rows)
    row_spec = lambda w: pl.BlockSpec((tm, w), lambda i: (i, 0))
    consts = [pw["w_branch"], pw["w_out"], pw["ffn_norm_g"], pw["w_ffn_in"], pw["w_ffn_out"]]
    return pl.pallas_call(
        functools.partial(_merge_ffn_kernel, final),
        grid=(rows // tm,),
        in_specs=([row_spec(D_MODEL), row_spec(3 * MIX), row_spec(GATE_W)] + [_layer_spec(c, l) for c in consts]
                  + [_layer_spec(pw["final_norm_g"], 0)]),
        out_specs=row_spec(D_MODEL),
        out_shape=jax.ShapeDtypeStruct((rows, D_MODEL), F32),
        compiler_params=pltpu.CompilerParams(dimension_semantics=("arbitrary",), vmem_limit_bytes=VMEM_LIMIT),
        name="merge_ffn",
    )(x, ys, gates, *consts, pw["final_norm_g"])


def _att_group(seq_len):
    return 2 if seq_len >= 2 * CHUNK else 1


def _band_bias(rel_bias, lq, grp):
    assert lq - 1 <= MAX_REL
    depth = rel_bias.shape[0]
    rtab = rel_bias[..., ::-1].astype(F32)
    n_far = ATT_WIN + lq - MAX_REL
    ext = jnp.concatenate([jnp.broadcast_to(rtab[..., :1], (depth, ATT_HEADS, n_far)),
                           rtab[..., 1:MAX_REL + lq]], axis=-1)
    wk = ATT_WIN + lq
    period = wk + lq - 1
    rot = jnp.concatenate([ext[..., lq - 1:], ext[..., :lq - 1]], axis=-1)
    chunk = jnp.tile(rot, (1, 1, lq))[..., :lq * (period - 1)].reshape(depth, ATT_HEADS, lq, period - 1)[..., :wk]
    blocks = [jnp.pad(chunk, ((0, 0), (0, 0), (0, 0), (j * lq, (grp - 1 - j) * lq)), constant_values=MASKED)
              for j in range(grp)]
    per_head = jnp.concatenate(blocks, axis=2)
    return per_head.reshape(depth, ATT_HEADS // 2, 2 * grp * lq, ATT_WIN + grp * lq)


def _prep_params(attn_norm_g, w_in, w_gate2, b_gate, gla_norm_g, pool_map, pool_scale, rel_bias, w_branch, w_out,
                 ffn_norm_g, w_ffn_in, w_ffn_out, final_norm_g, prompt_len, sample_len):
    depth = w_in.shape[0]
    o = [0, 512, 768, 1024, 1536, 2048, 2064, 2576, 3088, 3600, 6672]
    col = lambda a, b: w_in[:, :, o[a]:o[b]]
    w_mix = jnp.concatenate([col(0, 1), col(3, 4), col(4, 5), col(6, 7), col(7, 8), col(8, 9)], axis=2)
    w_gla = jnp.concatenate([col(1, 2), col(2, 3), col(5, 6),
                             jnp.zeros((depth, D_MODEL, LANES - GLA_RANK), w_in.dtype)], axis=2)
    wg2 = jnp.concatenate([w_gate2, jnp.zeros((depth, LANES - GLA_RANK, GLA_QK_W), w_gate2.dtype)], axis=1)
    vec = lambda a: a[:, None, :]
    return dict(
        attn_norm_g=vec(attn_norm_g), w_gates=col(9, 10).astype(BF16), w_mix=w_mix.astype(BF16),
        w_gla=w_gla.astype(BF16), w_gate2=wg2.astype(BF16), b_gate=vec(b_gate),
        gla_norm_g=vec(gla_norm_g), pool_map=pool_map.astype(BF16), pool_scale=vec(pool_scale),
        bias_p=_band_bias(rel_bias, CHUNK, _att_group(prompt_len)),
        bias_s=_band_bias(rel_bias, min(CHUNK, sample_len), _att_group(sample_len)),
        w_branch=w_branch.astype(BF16), w_out=w_out.astype(BF16), ffn_norm_g=vec(ffn_norm_g),
        w_ffn_in=w_ffn_in.astype(BF16), w_ffn_out=w_ffn_out.astype(BF16),
        final_norm_g=final_norm_g[None, None, :])


def _layer(x, n_seq, seq_len, state, pw, l, pos0, final):
    gates, mix, gla = _inproj(x, pw, l)
    if state is not None:
        hist, s0, kc, vc = state
        hist = jnp.pad(hist, ((0, 0), (HIST_ROWS - POOL_HIST, 0), (0, 0)))
        s0t = s0.reshape(n_seq, 2, 2, GLA_DK, GLA_DV).transpose(0, 1, 4, 2, 3).reshape(n_seq, 2, LANES, LANES)
        state = (hist, s0t, kc.reshape(n_seq, ATT_WIN, MIX), vc.reshape(n_seq, ATT_WIN, MIX))
    ys, s_fin_t, pool16, k_new, v_new = _mixers(mix, gla, state, pw, l, n_seq, seq_len, pos0)
    x_new = _merge_ffn(x, ys, gates, pw, l, final)

    keep = min(ATT_WIN, seq_len)
    gla_new = (s_fin_t.reshape(n_seq, 2, GLA_DV, 2, GLA_DK).transpose(0, 1, 3, 4, 2)
               .reshape(n_seq, GLA_HEADS, GLA_DK, GLA_DV))
    return x_new, (pool16[:, HIST_ROWS - POOL_HIST:], gla_new,
                   k_new.reshape(n_seq, keep, ATT_HEADS, ATT_DH), v_new.reshape(n_seq, keep, ATT_HEADS, ATT_DH))


def kernel(x_prompt, x_sample, cache_pool, state_gla, cache_k, cache_v, attn_norm_g, w_in, w_gate2, b_gate,
           gla_norm_g, pool_map, pool_scale, rel_bias, w_branch, w_out, ffn_norm_g, w_ffn_in, w_ffn_out,
           final_norm_g):
    batch, seq, _ = x_prompt.shape
    dec_batch, dec_seq, _ = x_sample.shape
    depth = w_in.shape[0]
    hp = x_prompt.reshape(batch * seq, D_MODEL)
    hs = x_sample.reshape(dec_batch * dec_seq, D_MODEL)
    pw = _prep_params(attn_norm_g, w_in, w_gate2, b_gate, gla_norm_g, pool_map, pool_scale, rel_bias, w_branch,
                      w_out, ffn_norm_g, w_ffn_in, w_ffn_out, final_norm_g, seq, dec_seq)
    outs_p, outs_s = [], []
    for l in range(depth):
        final = l == depth - 1
        hp, sp = _layer(hp, batch, seq, None, pw, l, 0, final)
        hs, ss = _layer(hs, dec_batch, dec_seq, (cache_pool[l], state_gla[l], cache_k[l], cache_v[l]), pw, l,
                        PAST_LEN, final)
        outs_p.append(sp)
        outs_s.append(ss)
    stack = lambda outs, j: jnp.stack([o[j] for o in outs])
    return (hp.reshape(batch, seq, D_MODEL), hs.reshape(dec_batch, dec_seq, D_MODEL),
            stack(outs_p, 0), stack(outs_s, 0), stack(outs_p, 1), stack(outs_s, 1),
            stack(outs_p, 2), stack(outs_s, 2), stack(outs_p, 3), stack(outs_s, 3))
```

```python
import functools

import jax
import jax.numpy as jnp
from jax import lax
from jax.experimental import pallas as pl
from jax.experimental.pallas import tpu as pltpu

F32 = jnp.float32
BF16 = jnp.bfloat16

D_MODEL = 1024
PAST_LEN = 2048
CHUNK = 64
MIX = 512
POOL_WINDOWS = (2, 4, 8, 16)
POOL_GW = 128
POOL_HIST = 15
HIST_ROWS = 16
GLA_HEADS = 4
GLA_DK = 64
GLA_DV = 128
GLA_QK_W = 256
GLA_RANK = 16
GLA_TAU = 16.0
GLA_SAFE_DECAY = 60.0
TOKEN_GROUP = 16
ATT_HEADS = 8
ATT_DH = 64
ATT_WIN = 512
MAX_REL = 128
N_BRANCH = 3
D_FF = 2816
EPS = 1e-6
MASKED = -1e30
LANES = 128

MIX_U, MIX_VB, MIX_RB, MIX_QC, MIX_KC, MIX_VC = 0, 512, 1024, 1536, 2048, 2560
MIX_W = 3072
GLA_Q, GLA_K, GLA_Z = 0, 256, 512
GLA_W = 640
GATE_W = N_BRANCH * D_MODEL
YS_A, YS_B, YS_C = 0, 512, 1024

ATT_STAGE_GAP = 1
MERGE_SPLIT = 2
ROW_TILE = 256
SEQ_TILE = 512
VMEM_LIMIT = 56 * 1024 * 1024

NT_DIMS = (((1,), (1,)), ((), ()))
TN_DIMS = (((0,), (0,)), ((), ()))


def _sigmoid(x):
    return 1.0 / (1.0 + jnp.exp(-x))


def _rms(x, g):
    return x * lax.rsqrt(jnp.mean(x * x, axis=-1, keepdims=True) + EPS) * g


def _layer_spec(arr, l):
    tail = (0,) * (arr.ndim - 1)
    return pl.BlockSpec((None,) + arr.shape[1:], lambda *_: (l,) + tail, pipeline_mode=pl.Buffered(1))


def _inproj_kernel(x_ref, g_ref, wg_ref, wm_ref, wl_ref, og_ref, om_ref, ol_ref):
    tm = x_ref.shape[0]
    for h in range(MERGE_SPLIT):
        rows = slice(h * (tm // MERGE_SPLIT), (h + 1) * (tm // MERGE_SPLIT))
        xb = _rms(x_ref[rows, :], g_ref[...]).astype(BF16)
        og_ref[rows, :] = jnp.dot(xb, wg_ref[...], preferred_element_type=F32)
        om_ref[rows, :] = jnp.dot(xb, wm_ref[...], preferred_element_type=F32)
        ol_ref[rows, :] = jnp.dot(xb, wl_ref[...], preferred_element_type=F32)


def _inproj(x, pw, l):
    rows = x.shape[0]
    tm = min(MERGE_SPLIT * ROW_TILE, rows)
    row_spec = lambda w: pl.BlockSpec((tm, w), lambda i: (i, 0))
    consts = [pw["attn_norm_g"], pw["w_gates"], pw["w_mix"], pw["w_gla"]]
    return pl.pallas_call(
        _inproj_kernel,
        grid=(rows // tm,),
        in_specs=[row_spec(D_MODEL)] + [_layer_spec(c, l) for c in consts],
        out_specs=[row_spec(GATE_W), row_spec(MIX_W), row_spec(GLA_W)],
        out_shape=[jax.ShapeDtypeStruct((rows, GATE_W), F32), jax.ShapeDtypeStruct((rows, MIX_W), F32),
                   jax.ShapeDtypeStruct((rows, GLA_W), F32)],
        compiler_params=pltpu.CompilerParams(dimension_semantics=("arbitrary",), vmem_limit_bytes=VMEM_LIMIT),
        name="in_proj",
    )(x, *consts)


def _mixer_kernel(tt, lg, lq, grp, has_state, pos0, *refs):
    refs = list(refs)
    mix_ref, gla_ref = refs[:2]
    k = 2
    if has_state:
        hist_ref, s0_ref, kc_ref, vc_ref = refs[k:k + 4]
        k += 4
    wg2_ref, bg_ref, gng_ref, pmap_ref, pscale_ref, bias_ref = refs[k:k + 6]
    k += 6
    ys_ref, sfin_ref, pool_ref, knew_ref, vnew_ref = refs[k:k + 5]
    carry_ref, st_ref, kbuf_ref, vbuf_ref, la_ref = refs[k + 5:]

    i = pl.program_id(1)
    lane = lax.broadcasted_iota(jnp.int32, (1, LANES), 1)
    head_mask = (lane < ATT_DH, lane >= ATT_DH)

    @pl.when(i == 0)
    def _init():
        if has_state:
            carry_ref[...] = hist_ref[0]
            st_ref[...] = s0_ref[0]
            kbuf_ref[0:ATT_WIN, :] = kc_ref[0].astype(BF16)
            vbuf_ref[0:ATT_WIN, :] = vc_ref[0].astype(BF16)
        else:
            carry_ref[...] = jnp.zeros_like(carry_ref)
            st_ref[...] = jnp.zeros_like(st_ref)
            kbuf_ref[0:ATT_WIN, :] = jnp.zeros((ATT_WIN, MIX), BF16)
            vbuf_ref[0:ATT_WIN, :] = jnp.zeros((ATT_WIN, MIX), BF16)

    u = mix_ref[:, MIX_U:MIX_U + MIX]
    ext = jnp.concatenate([carry_ref[...], u], axis=0)
    carry_ref[...] = u[tt - HIST_ROWS:, :]
    row = lax.broadcasted_iota(jnp.int32, (tt, POOL_GW), 0)
    n_seen = (pos0 + i * tt + row + 1).astype(F32)
    for g, w in enumerate(POOL_WINDOWS):
        cols = slice(g * POOL_GW, (g + 1) * POOL_GW)
        s = ext[:, cols]
        shift = 1
        while shift < w:
            s = s + pltpu.roll(s, shift, 0)
            shift *= 2
        p = s[HIST_ROWS:, :] / jnp.minimum(float(w), n_seen) - u[:, cols]
        y = jnp.dot(p.astype(BF16), pmap_ref[g], preferred_element_type=F32) * pscale_ref[:, cols]
        ys_ref[:, YS_A + g * POOL_GW:YS_A + (g + 1) * POOL_GW] = y.astype(BF16)

    z = gla_ref[:, GLA_Z:GLA_Z + LANES].astype(BF16)
    pre = jnp.dot(z, wg2_ref[...], preferred_element_type=F32) + bg_ref[...]
    log_a = -(jnp.maximum(-pre, 0.0) + jnp.log1p(jnp.exp(-jnp.abs(pre)))) / GLA_TAU
    row_in_blk = lax.broadcasted_iota(jnp.int32, (tt, GLA_QK_W), 0) & (lg - 1)
    b = log_a
    shift = 1
    while shift < lg:
        b = b + jnp.where(row_in_blk >= shift, pltpu.roll(b, shift, 0), 0.0)
        shift *= 2

    ri = lax.broadcasted_iota(jnp.int32, (2 * lg, lg), 0) & (lg - 1)
    ci = lax.broadcasted_iota(jnp.int32, (2 * lg, lg), 1)
    causal = ci <= ri

    def stack_heads(x):
        return jnp.concatenate([jnp.where(head_mask[0], x, 0.0), jnp.where(head_mask[1], x, 0.0)],
                               axis=0).astype(BF16)

    def gla_out(rows, h, o):
        r = mix_ref[rows, MIX_RB + h * GLA_DV:MIX_RB + (h + 1) * GLA_DV]
        y = _rms(o, gng_ref[...]) * (r * _sigmoid(r))
        ys_ref[rows, YS_B + h * GLA_DV:YS_B + (h + 1) * GLA_DV] = y.astype(BF16)

    blocks_safe = jnp.max(-b) <= GLA_SAFE_DECAY

    @pl.when(blocks_safe)
    def _gla_blocks():
        for p in range(GLA_HEADS // 2):
            lanes = slice(p * LANES, (p + 1) * LANES)
            bq = b[:, lanes]
            eb = jnp.exp(bq)
            enb = jnp.exp(-bq)
            qs = gla_ref[:, GLA_Q + p * LANES:GLA_Q + (p + 1) * LANES] * (GLA_DK ** -0.5) * eb
            kt = gla_ref[:, GLA_K + p * LANES:GLA_K + (p + 1) * LANES] * enb
            n_blk = tt // lg
            blk = lambda c: slice(c * lg, (c + 1) * lg)
            vb = [[mix_ref[blk(c), MIX_VB + (2 * p + hl) * GLA_DV:MIX_VB + (2 * p + hl + 1) * GLA_DV].astype(BF16)
                   for hl in range(2)] for c in range(n_blk)]
            e_last = [eb[(c + 1) * lg - 1:(c + 1) * lg, :] for c in range(n_blk)]
            qst = [stack_heads(qs[blk(c)]) for c in range(n_blk)]
            att, upd = [], []
            for c in range(n_blk):
                a = lax.dot_general(qst[c], kt[blk(c)].astype(BF16), NT_DIMS, preferred_element_type=F32)
                att.append(jnp.where(causal, a, 0.0).astype(BF16))
                kdb = (kt[blk(c)] * e_last[c]).astype(BF16)
                u2 = [lax.dot_general(vb[c][hl], kdb, TN_DIMS, preferred_element_type=F32) for hl in range(2)]
                upd.append(jnp.where(head_mask[0], u2[0], u2[1]))
            starts = [st_ref[p]]
            for c in range(n_blk):
                starts.append(e_last[c] * starts[c] + upd[c])
            st_ref[p] = starts[n_blk]
            for c in range(n_blk):
                inter = lax.dot_general(qst[c], starts[c].astype(BF16), NT_DIMS, preferred_element_type=F32)
                for hl in range(2):
                    half = slice(hl * lg, (hl + 1) * lg)
                    gla_out(blk(c), 2 * p + hl,
                            jnp.dot(att[c][half], vb[c][hl], preferred_element_type=F32) + inter[half])

    @pl.when(jnp.logical_not(blocks_safe))
    def _gla_tokens():
        la_ref[...] = log_a
        eye = (lax.broadcasted_iota(jnp.int32, (LANES, LANES), 0)
               == lax.broadcasted_iota(jnp.int32, (LANES, LANES), 1)).astype(F32)
        tok = lax.broadcasted_iota(jnp.int32, (TOKEN_GROUP, LANES), 0)
        for p in range(GLA_HEADS // 2):
            def group(gidx, st, p=p):
                rows = pl.ds(pl.multiple_of(gidx * TOKEN_GROUP, TOKEN_GROUP), TOKEN_GROUP)
                a = jnp.exp(la_ref[rows, p * LANES:(p + 1) * LANES])
                q = gla_ref[rows, GLA_Q + p * LANES:GLA_Q + (p + 1) * LANES] * (GLA_DK ** -0.5)
                k = gla_ref[rows, GLA_K + p * LANES:GLA_K + (p + 1) * LANES]
                v_t = [lax.dot_general(eye, mix_ref[rows, MIX_VB + (2 * p + hl) * GLA_DV:
                                                    MIX_VB + (2 * p + hl + 1) * GLA_DV],
                                       NT_DIMS, precision=lax.Precision.HIGHEST, preferred_element_type=F32)
                       for hl in range(2)]
                o = [jnp.zeros((TOKEN_GROUP, GLA_DV), F32), jnp.zeros((TOKEN_GROUP, GLA_DV), F32)]
                for t in range(TOKEN_GROUP):
                    v_col = jnp.where(head_mask[0], v_t[0][:, t:t + 1], v_t[1][:, t:t + 1])
                    st = a[t:t + 1, :] * st + v_col * k[t:t + 1, :]
                    stb = st.astype(BF16)
                    for hl in range(2):
                        q_t = jnp.where(tok == t, jnp.where(head_mask[hl], q, 0.0), 0.0).astype(BF16)
                        o[hl] = o[hl] + lax.dot_general(q_t, stb, NT_DIMS, preferred_element_type=F32)
                for hl in range(2):
                    gla_out(rows, 2 * p + hl, o[hl])
                return st

            st_ref[p] = lax.fori_loop(0, tt // TOKEN_GROUP, group, st_ref[p])

    kbuf_ref[ATT_WIN:ATT_WIN + tt, :] = mix_ref[:, MIX_KC:MIX_KC + MIX].astype(BF16)
    vbuf_ref[ATT_WIN:ATT_WIN + tt, :] = mix_ref[:, MIX_VC:MIX_VC + MIX].astype(BF16)
    gq = grp * lq
    win = ATT_WIN + gq
    kcol = lax.broadcasted_iota(jnp.int32, (1, win), 1)
    first_tile_neg = jnp.where(i == 0, MASKED, 0.0).astype(F32)

    def scores(gi, p):
        q2 = mix_ref[gi * gq:(gi + 1) * gq, MIX_QC + p * LANES:MIX_QC + (p + 1) * LANES] * (ATT_DH ** -0.5)
        s = lax.dot_general(stack_heads(q2), kbuf_ref[gi * gq:gi * gq + win, p * LANES:(p + 1) * LANES],
                            NT_DIMS, preferred_element_type=F32)
        s = s + bias_ref[p]
        if not has_state:
            s = s + jnp.where(kcol < ATT_WIN - gi * gq, first_tile_neg, 0.0)
        return s

    def weights(s):
        e = jnp.exp(s - jnp.max(s, axis=-1, keepdims=True))
        return e.astype(BF16), jnp.sum(e, axis=-1, keepdims=True)

    def attend(gi, p, e, den):
        o = jnp.dot(e, vbuf_ref[gi * gq:gi * gq + win, p * LANES:(p + 1) * LANES],
                    preferred_element_type=F32) / den
        ys_ref[gi * gq:(gi + 1) * gq, YS_C + p * LANES:YS_C + (p + 1) * LANES] = (
            jnp.where(head_mask[0], o[:gq], o[gq:]).astype(BF16))

    units = [(gi, p) for gi in range(tt // gq) for p in range(ATT_HEADS // 2)]
    s_q, w_q = {}, {}
    for t in range(len(units) + 2 * ATT_STAGE_GAP):
        if t < len(units):
            s_q[t] = scores(*units[t])
        if ATT_STAGE_GAP <= t < len(units) + ATT_STAGE_GAP:
            w_q[t - ATT_STAGE_GAP] = weights(s_q.pop(t - ATT_STAGE_GAP))
        if t >= 2 * ATT_STAGE_GAP:
            attend(*units[t - 2 * ATT_STAGE_GAP], *w_q.pop(t - 2 * ATT_STAGE_GAP))

    if not has_state:
        kbuf_ref[0:ATT_WIN, :] = kbuf_ref[tt:tt + ATT_WIN, :]
        vbuf_ref[0:ATT_WIN, :] = vbuf_ref[tt:tt + ATT_WIN, :]

    @pl.when(i == pl.num_programs(1) - 1)
    def _fin():
        keep = min(ATT_WIN, tt)
        sfin_ref[0] = st_ref[...]
        pool_ref[0] = mix_ref[tt - HIST_ROWS:, MIX_U:MIX_U + MIX]
        knew_ref[0] = mix_ref[tt - keep:, MIX_KC:MIX_KC + MIX]
        vnew_ref[0] = mix_ref[tt - keep:, MIX_VC:MIX_VC + MIX]


def _mixers(mix, gla, state, pw, l, n_seq, seq_len, pos0):
    has_state = state is not None
    tt = min(SEQ_TILE, seq_len)
    lg = min(CHUNK, seq_len)
    lq = min(CHUNK, seq_len)
    grp = _att_group(seq_len)
    nt = seq_len // tt
    assert has_state or tt == ATT_WIN
    kernel = functools.partial(_mixer_kernel, tt, lg, lq, grp, has_state, pos0)

    tile = lambda w: pl.BlockSpec((tt, w), lambda s, i: (s * nt + i, 0))
    in_specs = [tile(MIX_W), tile(GLA_W)]
    args = [mix, gla]
    if has_state:
        hist, s0t, kc, vc = state
        in_specs += [pl.BlockSpec((1, HIST_ROWS, MIX), lambda s, i: (s, 0, 0)),
                     pl.BlockSpec((1, 2, LANES, LANES), lambda s, i: (s, 0, 0, 0)),
                     pl.BlockSpec((1, ATT_WIN, MIX), lambda s, i: (s, 0, 0)),
                     pl.BlockSpec((1, ATT_WIN, MIX), lambda s, i: (s, 0, 0))]
        args += [hist, s0t, kc, vc]
    consts = [pw["w_gate2"], pw["b_gate"], pw["gla_norm_g"], pw["pool_map"], pw["pool_scale"],
              pw["bias_s"] if has_state else pw["bias_p"]]
    in_specs += [_layer_spec(c, l) for c in consts]
    args += consts
    keep = min(ATT_WIN, seq_len)
    per_seq = lambda r: pl.BlockSpec((1, r, MIX), lambda s, i: (s, 0, 0))

    return pl.pallas_call(
        kernel,
        grid=(n_seq, nt),
        in_specs=in_specs,
        out_specs=[pl.BlockSpec((tt, 3 * MIX), lambda s, i: (s * nt + i, 0)),
                   pl.BlockSpec((1, 2, LANES, LANES), lambda s, i: (s, 0, 0, 0)),
                   per_seq(HIST_ROWS), per_seq(keep), per_seq(keep)],
        out_shape=[jax.ShapeDtypeStruct((n_seq * seq_len, 3 * MIX), BF16),
                   jax.ShapeDtypeStruct((n_seq, 2, LANES, LANES), F32),
                   jax.ShapeDtypeStruct((n_seq, HIST_ROWS, MIX), F32),
                   jax.ShapeDtypeStruct((n_seq, keep, MIX), F32),
                   jax.ShapeDtypeStruct((n_seq, keep, MIX), F32)],
        scratch_shapes=[pltpu.VMEM((HIST_ROWS, MIX), F32),
                        pltpu.VMEM((2, LANES, LANES), F32),
                        pltpu.VMEM((ATT_WIN + tt, MIX), BF16),
                        pltpu.VMEM((ATT_WIN + tt, MIX), BF16),
                        pltpu.VMEM((tt, GLA_QK_W), F32)],
        compiler_params=pltpu.CompilerParams(dimension_semantics=("arbitrary", "arbitrary"),
                                             vmem_limit_bytes=VMEM_LIMIT),
        name="mixers_sample" if has_state else "mixers_prompt",
    )(*args)


def _merge_ffn_kernel(final, x_ref, ys_ref, hg_ref, wbr_ref, wout_ref, g2_ref, wf_ref, wo_ref, fg_ref, out_ref):
    tm = x_ref.shape[0]
    halves = [slice(h * (tm // MERGE_SPLIT), (h + 1) * (tm // MERGE_SPLIT)) for h in range(MERGE_SPLIT)]

    def merge(rows):
        merged = None
        for g in range(N_BRANCH):
            branch = jnp.dot(ys_ref[rows, g * MIX:(g + 1) * MIX], wbr_ref[g], preferred_element_type=F32)
            term = _sigmoid(hg_ref[rows, g * D_MODEL:(g + 1) * D_MODEL]) * branch
            merged = term if merged is None else merged + term
        return merged.astype(BF16)

    def residual(rows, merged):
        x1 = x_ref[rows, :] + jnp.dot(merged, wout_ref[...], preferred_element_type=F32)
        return x1, _rms(x1, g2_ref[...]).astype(BF16)

    def hidden(xn):
        a = jnp.dot(xn, wf_ref[:, :D_FF], preferred_element_type=F32)
        gt = jnp.dot(xn, wf_ref[:, D_FF:], preferred_element_type=F32)
        return (a * _sigmoid(a) * gt).astype(BF16)

    def finish(rows, x1, act):
        x2 = x1 + jnp.dot(act, wo_ref[...], preferred_element_type=F32)
        if final:
            x2 = _rms(x2, fg_ref[...])
        out_ref[rows, :] = x2

    merged = [merge(r) for r in halves]
    res = [residual(r, m) for r, m in zip(halves, merged)]
    acts = [hidden(xn) for _, xn in res]
    for r, (x1, _), act in zip(halves, res, acts):
        finish(r, x1, act)


def _merge_ffn(x, ys, gates, pw, l, final):
    rows = x.shape[0]
    tm = min(MERGE_SPLIT * ROW_TILE, rows)
    row_spec = lambda w: pl.BlockSpec((tm, w), lambda i: (i, 0))
    consts = [pw["w_branch"], pw["w_out"], pw["ffn_norm_g"], pw["w_ffn_in"], pw["w_ffn_out"]]
    return pl.pallas_call(
        functools.partial(_merge_ffn_kernel, final),
        grid=(rows // tm,),
        in_specs=([row_spec(D_MODEL), row_spec(3 * MIX), row_spec(GATE_W)] + [_layer_spec(c, l) for c in consts]
                  + [_layer_spec(pw["final_norm_g"], 0)]),
        out_specs=row_spec(D_MODEL),
        out_shape=jax.ShapeDtypeStruct((rows, D_MODEL), F32),
        compiler_params=pltpu.CompilerParams(dimension_semantics=("arbitrary",), vmem_limit_bytes=VMEM_LIMIT),
        name="merge_ffn",
    )(x, ys, gates, *consts, pw["final_norm_g"])


def _att_group(seq_len):
    return 2 if seq_len >= 2 * CHUNK else 1


def _band_bias(rel_bias, lq, grp):
    assert lq - 1 <= MAX_REL
    depth = rel_bias.shape[0]
    rtab = rel_bias[..., ::-1].astype(F32)
    n_far = ATT_WIN + lq - MAX_REL
    ext = jnp.concatenate([jnp.broadcast_to(rtab[..., :1], (depth, ATT_HEADS, n_far)),
                           rtab[..., 1:MAX_REL + lq]], axis=-1)
    wk = ATT_WIN + lq
    period = wk + lq - 1
    rot = jnp.concatenate([ext[..., lq - 1:], ext[..., :lq - 1]], axis=-1)
    chunk = jnp.tile(rot, (1, 1, lq))[..., :lq * (period - 1)].reshape(depth, ATT_HEADS, lq, period - 1)[..., :wk]
    blocks = [jnp.pad(chunk, ((0, 0), (0, 0), (0, 0), (j * lq, (grp - 1 - j) * lq)), constant_values=MASKED)
              for j in range(grp)]
    per_head = jnp.concatenate(blocks, axis=2)
    return per_head.reshape(depth, ATT_HEADS // 2, 2 * grp * lq, ATT_WIN + grp * lq)


def _prep_params(attn_norm_g, w_in, w_gate2, b_gate, gla_norm_g, pool_map, pool_scale, rel_bias, w_branch, w_out,
                 ffn_norm_g, w_ffn_in, w_ffn_out, final_norm_g, prompt_len, sample_len):
    depth = w_in.shape[0]
    o = [0, 512, 768, 1024, 1536, 2048, 2064, 2576, 3088, 3600, 6672]
    col = lambda a, b: w_in[:, :, o[a]:o[b]]
    w_mix = jnp.concatenate([col(0, 1), col(3, 4), col(4, 5), col(6, 7), col(7, 8), col(8, 9)], axis=2)
    w_gla = jnp.concatenate([col(1, 2), col(2, 3), col(5, 6),
                             jnp.zeros((depth, D_MODEL, LANES - GLA_RANK), w_in.dtype)], axis=2)
    wg2 = jnp.concatenate([w_gate2, jnp.zeros((depth, LANES - GLA_RANK, GLA_QK_W), w_gate2.dtype)], axis=1)
    vec = lambda a: a[:, None, :]
    return dict(
        attn_norm_g=vec(attn_norm_g), w_gates=col(9, 10).astype(BF16), w_mix=w_mix.astype(BF16),
        w_gla=w_gla.astype(BF16), w_gate2=wg2.astype(BF16), b_gate=vec(b_gate),
        gla_norm_g=vec(gla_norm_g), pool_map=pool_map.astype(BF16), pool_scale=vec(pool_scale),
        bias_p=_band_bias(rel_bias, CHUNK, _att_group(prompt_len)),
        bias_s=_band_bias(rel_bias, min(CHUNK, sample_len), _att_group(sample_len)),
        w_branch=w_branch.astype(BF16), w_out=w_out.astype(BF16), ffn_norm_g=vec(ffn_norm_g),
        w_ffn_in=w_ffn_in.astype(BF16), w_ffn_out=w_ffn_out.astype(BF16),
        final_norm_g=final_norm_g[None, None, :])


def _layer(x, n_seq, seq_len, state, pw, l, pos0, final):
    gates, mix, gla = _inproj(x, pw, l)
    if state is not None:
        hist, s0, kc, vc = state
        hist = jnp.pad(hist, ((0, 0), (HIST_ROWS - POOL_HIST, 0), (0, 0)))
        s0t = s0.reshape(n_seq, 2, 2, GLA_DK, GLA_DV).transpose(0, 1, 4, 2, 3).reshape(n_seq, 2, LANES, LANES)
        state = (hist, s0t, kc.reshape(n_seq, ATT_WIN, MIX), vc.reshape(n_seq, ATT_WIN, MIX))
    ys, s_fin_t, pool16, k_new, v_new = _mixers(mix, gla, state, pw, l, n_seq, seq_len, pos0)
    x_new = _merge_ffn(x, ys, gates, pw, l, final)

    keep = min(ATT_WIN, seq_len)
    gla_new = (s_fin_t.reshape(n_seq, 2, GLA_DV, 2, GLA_DK).transpose(0, 1, 3, 4, 2)
               .reshape(n_seq, GLA_HEADS, GLA_DK, GLA_DV))
    return x_new, (pool16[:, HIST_ROWS - POOL_HIST:], gla_new,
                   k_new.reshape(n_seq, keep, ATT_HEADS, ATT_DH), v_new.reshape(n_seq, keep, ATT_HEADS, ATT_DH))


def kernel(x_prompt, x_sample, cache_pool, state_gla, cache_k, cache_v, attn_norm_g, w_in, w_gate2, b_gate,
           gla_norm_g, pool_map, pool_scale, rel_bias, w_branch, w_out, ffn_norm_g, w_ffn_in, w_ffn_out,
           final_norm_g):
    batch, seq, _ = x_prompt.shape
    dec_batch, dec_seq, _ = x_sample.shape
    depth = w_in.shape[0]
    hp = x_prompt.reshape(batch * seq, D_MODEL)
    hs = x_sample.reshape(dec_batch * dec_seq, D_MODEL)
    pw = _prep_params(attn_norm_g, w_in, w_gate2, b_gate, gla_norm_g, pool_map, pool_scale, rel_bias, w_branch,
                      w_out, ffn_norm_g, w_ffn_in, w_ffn_out, final_norm_g, seq, dec_seq)
    outs_p, outs_s = [], []
    for l in range(depth):
        final = l == depth - 1
        hp, sp = _layer(hp, batch, seq, None, pw, l, 0, final)
        hs, ss = _layer(hs, dec_batch, dec_seq, (cache_pool[l], state_gla[l], cache_k[l], cache_v[l]), pw, l,
                        PAST_LEN, final)
        outs_p.append(sp)
        outs_s.append(ss)
    stack = lambda outs, j: jnp.stack([o[j] for o in outs])
    return (hp.reshape(batch, seq, D_MODEL), hs.reshape(dec_batch, dec_seq, D_MODEL),
            stack(outs_p, 0), stack(outs_s, 0), stack(outs_p, 1), stack(outs_s, 1),
            stack(outs_p, 2), stack(outs_s, 2), stack(outs_p, 3), stack(outs_s, 3))
```

```python
import functools

import jax
import jax.numpy as jnp
from jax import lax
from jax.experimental import pallas as pl
from jax.experimental.pallas import tpu as pltpu

F32 = jnp.float32
BF16 = jnp.bfloat16

D_MODEL = 1024
PAST_LEN = 2048
CHUNK = 64
MIX = 512
POOL_WINDOWS = (2, 4, 8, 16)
POOL_GW = 128
POOL_HIST = 15
HIST_ROWS = 16
GLA_HEADS = 4
GLA_DK = 64
GLA_DV = 128
GLA_QK_W = 256
GLA_RANK = 16
GLA_TAU = 16.0
GLA_SAFE_DECAY = 60.0
TOKEN_GROUP = 16
ATT_HEADS = 8
ATT_DH = 64
ATT_WIN = 512
MAX_REL = 128
N_BRANCH = 3
D_FF = 2816
EPS = 1e-6
MASKED = -1e30
LANES = 128

MIX_U, MIX_VB, MIX_RB, MIX_QC, MIX_KC, MIX_VC = 0, 512, 1024, 1536, 2048, 2560
MIX_W = 3072
GLA_Q, GLA_K, GLA_Z = 0, 256, 512
GLA_W = 640
GATE_W = N_BRANCH * D_MODEL
YS_A, YS_B, YS_C = 0, 512, 1024

ATT_STAGE_GAP = 1
MERGE_SPLIT = 2
ROW_TILE = 256
SEQ_TILE = 512
VMEM_LIMIT = 56 * 1024 * 1024

NT_DIMS = (((1,), (1,)), ((), ()))
TN_DIMS = (((0,), (0,)), ((), ()))


def _sigmoid(x):
    return 1.0 / (1.0 + jnp.exp(-x))


def _rms(x, g):
    return x * lax.rsqrt(jnp.mean(x * x, axis=-1, keepdims=True) + EPS) * g


def _layer_spec(arr, l):
    tail = (0,) * (arr.ndim - 1)
    return pl.BlockSpec((None,) + arr.shape[1:], lambda *_: (l,) + tail, pipeline_mode=pl.Buffered(1))


def _inproj_kernel(x_ref, g_ref, wg_ref, wm_ref, wl_ref, og_ref, om_ref, ol_ref):
    tm = x_ref.shape[0]
    for h in range(MERGE_SPLIT):
        rows = slice(h * (tm // MERGE_SPLIT), (h + 1) * (tm // MERGE_SPLIT))
        xb = _rms(x_ref[rows, :], g_ref[...]).astype(BF16)
        og_ref[rows, :] = jnp.dot(xb, wg_ref[...], preferred_element_type=F32)
        om_ref[rows, :] = jnp.dot(xb, wm_ref[...], preferred_element_type=F32)
        ol_ref[rows, :] = jnp.dot(xb, wl_ref[...], preferred_element_type=F32)


def _inproj(x, pw, l):
    rows = x.shape[0]
    tm = min(MERGE_SPLIT * ROW_TILE, rows)
    row_spec = lambda w: pl.BlockSpec((tm, w), lambda i: (i, 0))
    consts = [pw["attn_norm_g"], pw["w_gates"], pw["w_mix"], pw["w_gla"]]
    return pl.pallas_call(
        _inproj_kernel,
        grid=(rows // tm,),
        in_specs=[row_spec(D_MODEL)] + [_layer_spec(c, l) for c in consts],
        out_specs=[row_spec(GATE_W), row_spec(MIX_W), row_spec(GLA_W)],
        out_shape=[jax.ShapeDtypeStruct((rows, GATE_W), F32), jax.ShapeDtypeStruct((rows, MIX_W), F32),
                   jax.ShapeDtypeStruct((rows, GLA_W), F32)],
        compiler_params=pltpu.CompilerParams(dimension_semantics=("arbitrary",), vmem_limit_bytes=VMEM_LIMIT),
        name="in_proj",
    )(x, *consts)


def _mixer_kernel(tt, lg, lq, grp, has_state, pos0, *refs):
    refs = list(refs)
    mix_ref, gla_ref = refs[:2]
    k = 2
    if has_state:
        hist_ref, s0_ref, kc_ref, vc_ref = refs[k:k + 4]
        k += 4
    wg2_ref, bg_ref, gng_ref, pmap_ref, pscale_ref, bias_ref = refs[k:k + 6]
    k += 6
    ys_ref, sfin_ref, pool_ref, knew_ref, vnew_ref = refs[k:k + 5]
    carry_ref, st_ref, kbuf_ref, vbuf_ref, la_ref = refs[k + 5:]

    i = pl.program_id(1)
    lane = lax.broadcasted_iota(jnp.int32, (1, LANES), 1)
    head_mask = (lane < ATT_DH, lane >= ATT_DH)

    @pl.when(i == 0)
    def _init():
        if has_state:
            carry_ref[...] = hist_ref[0]
            st_ref[...] = s0_ref[0]
            kbuf_ref[0:ATT_WIN, :] = kc_ref[0].astype(BF16)
            vbuf_ref[0:ATT_WIN, :] = vc_ref[0].astype(BF16)
        else:
            carry_ref[...] = jnp.zeros_like(carry_ref)
            st_ref[...] = jnp.zeros_like(st_ref)
            kbuf_ref[0:ATT_WIN, :] = jnp.zeros((ATT_WIN, MIX), BF16)
            vbuf_ref[0:ATT_WIN, :] = jnp.zeros((ATT_WIN, MIX), BF16)

    u = mix_ref[:, MIX_U:MIX_U + MIX]
    ext = jnp.concatenate([carry_ref[...], u], axis=0)
    carry_ref[...] = u[tt - HIST_ROWS:, :]
    row = lax.broadcasted_iota(jnp.int32, (tt, POOL_GW), 0)
    n_seen = (pos0 + i * tt + row + 1).astype(F32)

    def pool_group(g, w):
        cols = slice(g * POOL_GW, (g + 1) * POOL_GW)
        s = ext[:, cols]
        shift = 1
        while shift < w:
            s = s + pltpu.roll(s, shift, 0)
            shift *= 2
        p = s[HIST_ROWS:, :] / jnp.minimum(float(w), n_seen) - u[:, cols]
        y = jnp.dot(p.astype(BF16), pmap_ref[g], preferred_element_type=F32) * pscale_ref[:, cols]
        ys_ref[:, YS_A + g * POOL_GW:YS_A + (g + 1) * POOL_GW] = y.astype(BF16)

    gla_in = {}
    row_in_blk = lax.broadcasted_iota(jnp.int32, (tt, GLA_QK_W), 0) & (lg - 1)

    def decay_logits():
        z = gla_ref[:, GLA_Z:GLA_Z + LANES].astype(BF16)
        pre = jnp.dot(z, wg2_ref[...], preferred_element_type=F32) + bg_ref[...]
        gla_in["log_a"] = -(jnp.maximum(-pre, 0.0) + jnp.log1p(jnp.exp(-jnp.abs(pre)))) / GLA_TAU
        gla_in["b"] = gla_in["log_a"]

    def scan_step(shift):
        b = gla_in["b"]
        gla_in["b"] = b + jnp.where(row_in_blk >= shift, pltpu.roll(b, shift, 0), 0.0)

    side_jobs = [functools.partial(pool_group, g, w) for g, w in enumerate(POOL_WINDOWS)]
    side_jobs += [decay_logits] + [functools.partial(scan_step, 1 << k) for k in range(lg.bit_length() - 1)]

    ri = lax.broadcasted_iota(jnp.int32, (2 * lg, lg), 0) & (lg - 1)
    ci = lax.broadcasted_iota(jnp.int32, (2 * lg, lg), 1)
    causal = ci <= ri

    def stack_heads(x):
        return jnp.concatenate([jnp.where(head_mask[0], x, 0.0), jnp.where(head_mask[1], x, 0.0)],
                               axis=0).astype(BF16)

    def gla_out(rows, h, o):
        r = mix_ref[rows, MIX_RB + h * GLA_DV:MIX_RB + (h + 1) * GLA_DV]
        y = _rms(o, gng_ref[...]) * (r * _sigmoid(r))
        ys_ref[rows, YS_B + h * GLA_DV:YS_B + (h + 1) * GLA_DV] = y.astype(BF16)

    kbuf_ref[ATT_WIN:ATT_WIN + tt, :] = mix_ref[:, MIX_KC:MIX_KC + MIX].astype(BF16)
    vbuf_ref[ATT_WIN:ATT_WIN + tt, :] = mix_ref[:, MIX_VC:MIX_VC + MIX].astype(BF16)
    gq = grp * lq
    win = ATT_WIN + gq
    kcol = lax.broadcasted_iota(jnp.int32, (1, win), 1)
    first_tile_neg = jnp.where(i == 0, MASKED, 0.0).astype(F32)

    def scores(gi, p):
        q2 = mix_ref[gi * gq:(gi + 1) * gq, MIX_QC + p * LANES:MIX_QC + (p + 1) * LANES] * (ATT_DH ** -0.5)
        s = lax.dot_general(stack_heads(q2), kbuf_ref[gi * gq:gi * gq + win, p * LANES:(p + 1) * LANES],
                            NT_DIMS, preferred_element_type=F32)
        s = s + bias_ref[p]
        if not has_state:
            s = s + jnp.where(kcol < ATT_WIN - gi * gq, first_tile_neg, 0.0)
        return s

    def weights(s):
        e = jnp.exp(s - jnp.max(s, axis=-1, keepdims=True))
        return e.astype(BF16), jnp.sum(e, axis=-1, keepdims=True)

    def attend(gi, p, e, den):
        o = jnp.dot(e, vbuf_ref[gi * gq:gi * gq + win, p * LANES:(p + 1) * LANES],
                    preferred_element_type=F32) / den
        ys_ref[gi * gq:(gi + 1) * gq, YS_C + p * LANES:YS_C + (p + 1) * LANES] = (
            jnp.where(head_mask[0], o[:gq], o[gq:]).astype(BF16))

    units = [(gi, p) for gi in range(tt // gq) for p in range(ATT_HEADS // 2)]
    s_q, w_q = {}, {}
    for t in range(max(len(units) + 2 * ATT_STAGE_GAP, len(side_jobs))):
        if t < len(side_jobs):
            side_jobs[t]()
        if t < len(units):
            s_q[t] = scores(*units[t])
        if ATT_STAGE_GAP <= t < len(units) + ATT_STAGE_GAP:
            w_q[t - ATT_STAGE_GAP] = weights(s_q.pop(t - ATT_STAGE_GAP))
        if 2 * ATT_STAGE_GAP <= t < len(units) + 2 * ATT_STAGE_GAP:
            attend(*units[t - 2 * ATT_STAGE_GAP], *w_q.pop(t - 2 * ATT_STAGE_GAP))

    if not has_state:
        kbuf_ref[0:ATT_WIN, :] = kbuf_ref[tt:tt + ATT_WIN, :]
        vbuf_ref[0:ATT_WIN, :] = vbuf_ref[tt:tt + ATT_WIN, :]

    log_a, b = gla_in["log_a"], gla_in["b"]
    blocks_safe = jnp.max(-b) <= GLA_SAFE_DECAY

    @pl.when(blocks_safe)
    def _gla_blocks():
        for p in range(GLA_HEADS // 2):
            lanes = slice(p * LANES, (p + 1) * LANES)
            bq = b[:, lanes]
            eb = jnp.exp(bq)
            enb = jnp.exp(-bq)
            qs = gla_ref[:, GLA_Q + p * LANES:GLA_Q + (p + 1) * LANES] * (GLA_DK ** -0.5) * eb
            kt = gla_ref[:, GLA_K + p * LANES:GLA_K + (p + 1) * LANES] * enb
            n_blk = tt // lg
            blk = lambda c: slice(c * lg, (c + 1) * lg)
            vb = [[mix_ref[blk(c), MIX_VB + (2 * p + hl) * GLA_DV:MIX_VB + (2 * p + hl + 1) * GLA_DV].astype(BF16)
                   for hl in range(2)] for c in range(n_blk)]
            e_last = [eb[(c + 1) * lg - 1:(c + 1) * lg, :] for c in range(n_blk)]
            qst = [stack_heads(qs[blk(c)]) for c in range(n_blk)]
            att, upd = [], []
            for c in range(n_blk):
                a = lax.dot_general(qst[c], kt[blk(c)].astype(BF16), NT_DIMS, preferred_element_type=F32)
                att.append(jnp.where(causal, a, 0.0).astype(BF16))
                kdb = (kt[blk(c)] * e_last[c]).astype(BF16)
                u2 = [lax.dot_general(vb[c][hl], kdb, TN_DIMS, preferred_element_type=F32) for hl in range(2)]
                upd.append(jnp.where(head_mask[0], u2[0], u2[1]))
            starts = [st_ref[p]]
            for c in range(n_blk):
                starts.append(e_last[c] * starts[c] + upd[c])
            st_ref[p] = starts[n_blk]
            for c in range(n_blk):
                inter = lax.dot_general(qst[c], starts[c].astype(BF16), NT_DIMS, preferred_element_type=F32)
                for hl in range(2):
                    half = slice(hl * lg, (hl + 1) * lg)
                    gla_out(blk(c), 2 * p + hl,
                            jnp.dot(att[c][half], vb[c][hl], preferred_element_type=F32) + inter[half])

    @pl.when(jnp.logical_not(blocks_safe))
    def _gla_tokens():
        la_ref[...] = log_a
        eye = (lax.broadcasted_iota(jnp.int32, (LANES, LANES), 0)
               == lax.broadcasted_iota(jnp.int32, (LANES, LANES), 1)).astype(F32)
        tok = lax.broadcasted_iota(jnp.int32, (TOKEN_GROUP, LANES), 0)
        for p in range(GLA_HEADS // 2):
            def group(gidx, st, p=p):
                rows = pl.ds(pl.multiple_of(gidx * TOKEN_GROUP, TOKEN_GROUP), TOKEN_GROUP)
                a = jnp.exp(la_ref[rows, p * LANES:(p + 1) * LANES])
                q = gla_ref[rows, GLA_Q + p * LANES:GLA_Q + (p + 1) * LANES] * (GLA_DK ** -0.5)
                k = gla_ref[rows, GLA_K + p * LANES:GLA_K + (p + 1) * LANES]
                v_t = [lax.dot_general(eye, mix_ref[rows, MIX_VB + (2 * p + hl) * GLA_DV:
                                                    MIX_VB + (2 * p + hl + 1) * GLA_DV],
                                       NT_DIMS, precision=lax.Precision.HIGHEST, preferred_element_type=F32)
                       for hl in range(2)]
                o = [jnp.zeros((TOKEN_GROUP, GLA_DV), F32), jnp.zeros((TOKEN_GROUP, GLA_DV), F32)]
                for t in range(TOKEN_GROUP):
                    v_col = jnp.where(head_mask[0], v_t[0][:, t:t + 1], v_t[1][:, t:t + 1])
                    st = a[t:t + 1, :] * st + v_col * k[t:t + 1, :]
                    stb = st.astype(BF16)
                    for hl in range(2):
                        q_t = jnp.where(tok == t, jnp.where(head_mask[hl], q, 0.0), 0.0).astype(BF16)
                        o[hl] = o[hl] + lax.dot_general(q_t, stb, NT_DIMS, preferred_element_type=F32)
                for hl in range(2):
                    gla_out(rows, 2 * p + hl, o[hl])
                return st

            st_ref[p] = lax.fori_loop(0, tt // TOKEN_GROUP, group, st_ref[p])

    @pl.when(i == pl.num_programs(1) - 1)
    def _fin():
        keep = min(ATT_WIN, tt)
        sfin_ref[0] = st_ref[...]
        pool_ref[0] = mix_ref[tt - HIST_ROWS:, MIX_U:MIX_U + MIX]
        knew_ref[0] = mix_ref[tt - keep:, MIX_KC:MIX_KC + MIX]
        vnew_ref[0] = mix_ref[tt - keep:, MIX_VC:MIX_VC + MIX]


def _mixers(mix, gla, state, pw, l, n_seq, seq_len, pos0):
    has_state = state is not None
    tt = min(SEQ_TILE, seq_len)
    lg = min(CHUNK, seq_len)
    lq = min(CHUNK, seq_len)
    grp = _att_group(seq_len)
    nt = seq_len // tt
    assert has_state or tt == ATT_WIN
    kernel = functools.partial(_mixer_kernel, tt, lg, lq, grp, has_state, pos0)

    tile = lambda w: pl.BlockSpec((tt, w), lambda s, i: (s * nt + i, 0))
    in_specs = [tile(MIX_W), tile(GLA_W)]
    args = [mix, gla]
    if has_state:
        hist, s0t, kc, vc = state
        in_specs += [pl.BlockSpec((1, HIST_ROWS, MIX), lambda s, i: (s, 0, 0)),
                     pl.BlockSpec((1, 2, LANES, LANES), lambda s, i: (s, 0, 0, 0)),
                     pl.BlockSpec((1, ATT_WIN, MIX), lambda s, i: (s, 0, 0)),
                     pl.BlockSpec((1, ATT_WIN, MIX), lambda s, i: (s, 0, 0))]
        args += [hist, s0t, kc, vc]
    consts = [pw["w_gate2"], pw["b_gate"], pw["gla_norm_g"], pw["pool_map"], pw["pool_scale"],
              pw["bias_s"] if has_state else pw["bias_p"]]
    in_specs += [_layer_spec(c, l) for c in consts]
    args += consts
    keep = min(ATT_WIN, seq_len)
    per_seq = lambda r: pl.BlockSpec((1, r, MIX), lambda s, i: (s, 0, 0))

    return pl.pallas_call(
        kernel,
        grid=(n_seq, nt),
        in_specs=in_specs,
        out_specs=[pl.BlockSpec((tt, 3 * MIX), lambda s, i: (s * nt + i, 0)),
                   pl.BlockSpec((1, 2, LANES, LANES), lambda s, i: (s, 0, 0, 0)),
                   per_seq(HIST_ROWS), per_seq(keep), per_seq(keep)],
        out_shape=[jax.ShapeDtypeStruct((n_seq * seq_len, 3 * MIX), BF16),
                   jax.ShapeDtypeStruct((n_seq, 2, LANES, LANES), F32),
                   jax.ShapeDtypeStruct((n_seq, HIST_ROWS, MIX), F32),
                   jax.ShapeDtypeStruct((n_seq, keep, MIX), F32),
                   jax.ShapeDtypeStruct((n_seq, keep, MIX), F32)],
        scratch_shapes=[pltpu.VMEM((HIST_ROWS, MIX), F32),
                        pltpu.VMEM((2, LANES, LANES), F32),
                        pltpu.VMEM((ATT_WIN + tt, MIX), BF16),
                        pltpu.VMEM((ATT_WIN + tt, MIX), BF16),
                        pltpu.VMEM((tt, GLA_QK_W), F32)],
        compiler_params=pltpu.CompilerParams(dimension_semantics=("arbitrary", "arbitrary"),
                                             vmem_limit_bytes=VMEM_LIMIT),
        name="mixers_sample" if has_state else "mixers_prompt",
    )(*args)


def _merge_ffn_kernel(final, x_ref, ys_ref, hg_ref, wbr_ref, wout_ref, g2_ref, wf_ref, wo_ref, fg_ref, out_ref):
    tm = x_ref.shape[0]
    halves = [slice(h * (tm // MERGE_SPLIT), (h + 1) * (tm // MERGE_SPLIT)) for h in range(MERGE_SPLIT)]

    def merge(rows):
        merged = None
        for g in range(N_BRANCH):
            branch = jnp.dot(ys_ref[rows, g * MIX:(g + 1) * MIX], wbr_ref[g], preferred_element_type=F32)
            term = _sigmoid(hg_ref[rows, g * D_MODEL:(g + 1) * D_MODEL]) * branch
            merged = term if merged is None else merged + term
        return merged.astype(BF16)

    def residual(rows, merged):
        x1 = x_ref[rows, :] + jnp.dot(merged, wout_ref[...], preferred_element_type=F32)
        return x1, _rms(x1, g2_ref[...]).astype(BF16)

    def hidden(xn):
        a = jnp.dot(xn, wf_ref[:, :D_FF], preferred_element_type=F32)
        gt = jnp.dot(xn, wf_ref[:, D_FF:], preferred_element_type=F32)
        return (a * _sigmoid(a) * gt).astype(BF16)

    def finish(rows, x1, act):
        x2 = x1 + jnp.dot(act, wo_ref[...], preferred_element_type=F32)
        if final:
            x2 = _rms(x2, fg_ref[...])
        out_ref[rows, :] = x2

    merged = [merge(r) for r in halves]
    res = [residual(r, m) for r, m in zip(halves, merged)]
    acts = [hidden(xn) for _, xn in res]
    for r, (x1, _), act in zip(halves, res, acts):
        finish(r, x1, act)


def _merge_ffn(x, ys, gates, pw, l, final):
    rows = x.shape[0]
    tm = min(MERGE_SPLIT * ROW_TILE, rows)
    row_spec = lambda w: pl.BlockSpec((tm, w), lambda i: (i, 0))
    consts = [pw["w_branch"], pw["w_out"], pw["ffn_norm_g"], pw["w_ffn_in"], pw["w_ffn_out"]]
    return pl.pallas_call(
        functools.partial(_merge_ffn_kernel, final),
        grid=(rows // tm,),
        in_specs=([row_spec(D_MODEL), row_spec(3 * MIX), row_spec(GATE_W)] + [_layer_spec(c, l) for c in consts]
                  + [_layer_spec(pw["final_norm_g"], 0)]),
        out_specs=row_spec(D_MODEL),
        out_shape=jax.ShapeDtypeStruct((rows, D_MODEL), F32),
        compiler_params=pltpu.CompilerParams(dimension_semantics=("arbitrary",), vmem_limit_bytes=VMEM_LIMIT),
        name="merge_ffn",
    )(x, ys, gates, *consts, pw["final_norm_g"])


def _att_group(seq_len):
    return 2 if seq_len >= 2 * CHUNK else 1


def _band_bias(rel_bias, lq, grp):
    assert lq - 1 <= MAX_REL
    depth = rel_bias.shape[0]
    rtab = rel_bias[..., ::-1].astype(F32)
    n_far = ATT_WIN + lq - MAX_REL
    ext = jnp.concatenate([jnp.broadcast_to(rtab[..., :1], (depth, ATT_HEADS, n_far)),
                           rtab[..., 1:MAX_REL + lq]], axis=-1)
    wk = ATT_WIN + lq
    period = wk + lq - 1
    rot = jnp.concatenate([ext[..., lq - 1:], ext[..., :lq - 1]], axis=-1)
    chunk = jnp.tile(rot, (1, 1, lq))[..., :lq * (period - 1)].reshape(depth, ATT_HEADS, lq, period - 1)[..., :wk]
    blocks = [jnp.pad(chunk, ((0, 0), (0, 0), (0, 0), (j * lq, (grp - 1 - j) * lq)), constant_values=MASKED)
              for j in range(grp)]
    per_head = jnp.concatenate(blocks, axis=2)
    return per_head.reshape(depth, ATT_HEADS // 2, 2 * grp * lq, ATT_WIN + grp * lq)


def _prep_params(attn_norm_g, w_in, w_gate2, b_gate, gla_norm_g, pool_map, pool_scale, rel_bias, w_branch, w_out,
                 ffn_norm_g, w_ffn_in, w_ffn_out, final_norm_g, prompt_len, sample_len):
    depth = w_in.shape[0]
    o = [0, 512, 768, 1024, 1536, 2048, 2064, 2576, 3088, 3600, 6672]
    col = lambda a, b: w_in[:, :, o[a]:o[b]]
    w_mix = jnp.concatenate([col(0, 1), col(3, 4), col(4, 5), col(6, 7), col(7, 8), col(8, 9)], axis=2)
    w_gla = jnp.concatenate([col(1, 2), col(2, 3), col(5, 6),
                             jnp.zeros((depth, D_MODEL, LANES - GLA_RANK), w_in.dtype)], axis=2)
    wg2 = jnp.concatenate([w_gate2, jnp.zeros((depth, LANES - GLA_RANK, GLA_QK_W), w_gate2.dtype)], axis=1)
    vec = lambda a: a[:, None, :]
    return dict(
        attn_norm_g=vec(attn_norm_g), w_gates=col(9, 10).astype(BF16), w_mix=w_mix.astype(BF16),
        w_gla=w_gla.astype(BF16), w_gate2=wg2.astype(BF16), b_gate=vec(b_gate),
        gla_norm_g=vec(gla_norm_g), pool_map=pool_map.astype(BF16), pool_scale=vec(pool_scale),
        bias_p=_band_bias(rel_bias, CHUNK, _att_group(prompt_len)),
        bias_s=_band_bias(rel_bias, min(CHUNK, sample_len), _att_group(sample_len)),
        w_branch=w_branch.astype(BF16), w_out=w_out.astype(BF16), ffn_norm_g=vec(ffn_norm_g),
        w_ffn_in=w_ffn_in.astype(BF16), w_ffn_out=w_ffn_out.astype(BF16),
        final_norm_g=final_norm_g[None, None, :])


def _layer(x, n_seq, seq_len, state, pw, l, pos0, final):
    gates, mix, gla = _inproj(x, pw, l)
    if state is not None:
        hist, s0, kc, vc = state
        hist = jnp.pad(hist, ((0, 0), (HIST_ROWS - POOL_HIST, 0), (0, 0)))
        s0t = s0.reshape(n_seq, 2, 2, GLA_DK, GLA_DV).transpose(0, 1, 4, 2, 3).reshape(n_seq, 2, LANES, LANES)
        state = (hist, s0t, kc.reshape(n_seq, ATT_WIN, MIX), vc.reshape(n_seq, ATT_WIN, MIX))
    ys, s_fin_t, pool16, k_new, v_new = _mixers(mix, gla, state, pw, l, n_seq, seq_len, pos0)
    x_new = _merge_ffn(x, ys, gates, pw, l, final)

    keep = min(ATT_WIN, seq_len)
    gla_new = (s_fin_t.reshape(n_seq, 2, GLA_DV, 2, GLA_DK).transpose(0, 1, 3, 4, 2)
               .reshape(n_seq, GLA_HEADS, GLA_DK, GLA_DV))
    return x_new, (pool16[:, HIST_ROWS - POOL_HIST:], gla_new,
                   k_new.reshape(n_seq, keep, ATT_HEADS, ATT_DH), v_new.reshape(n_seq, keep, ATT_HEADS, ATT_DH))


def kernel(x_prompt, x_sample, cache_pool, state_gla, cache_k, cache_v, attn_norm_g, w_in, w_gate2, b_gate,
           gla_norm_g, pool_map, pool_scale, rel_bias, w_branch, w_out, ffn_norm_g, w_ffn_in, w_ffn_out,
           final_norm_g):
    batch, seq, _ = x_prompt.shape
    dec_batch, dec_seq, _ = x_sample.shape
    depth = w_in.shape[0]
    hp = x_prompt.reshape(batch * seq, D_MODEL)
    hs = x_sample.reshape(dec_batch * dec_seq, D_MODEL)
    pw = _prep_params(attn_norm_g, w_in, w_gate2, b_gate, gla_norm_g, pool_map, pool_scale, rel_bias, w_branch,
                      w_out, ffn_norm_g, w_ffn_in, w_ffn_out, final_norm_g, seq, dec_seq)
    outs_p, outs_s = [], []
    for l in range(depth):
        final = l == depth - 1
        hp, sp = _layer(hp, batch, seq, None, pw, l, 0, final)
        hs, ss = _layer(hs, dec_batch, dec_seq, (cache_pool[l], state_gla[l], cache_k[l], cache_v[l]), pw, l,
                        PAST_LEN, final)
        outs_p.append(sp)
        outs_s.append(ss)
    stack = lambda outs, j: jnp.stack([o[j] for o in outs])
    return (hp.reshape(batch, seq, D_MODEL), hs.reshape(dec_batch, dec_seq, D_MODEL),
            stack(outs_p, 0), stack(outs_s, 0), stack(outs_p, 1), stack(outs_s, 1),
            stack(outs_p, 2), stack(outs_s, 2), stack(outs_p, 3), stack(outs_s, 3))
```

```python
import functools

import jax
import jax.numpy as jnp
from jax import lax
from jax.experimental import pallas as pl
from jax.experimental.pallas import tpu as pltpu

F32 = jnp.float32
BF16 = jnp.bfloat16

D_MODEL = 1024
PAST_LEN = 2048
CHUNK = 64
MIX = 512
POOL_WINDOWS = (2, 4, 8, 16)
POOL_GW = 128
POOL_HIST = 15
HIST_ROWS = 16
GLA_HEADS = 4
GLA_DK = 64
GLA_DV = 128
GLA_QK_W = 256
GLA_RANK = 16
GLA_TAU = 16.0
GLA_SAFE_DECAY = 60.0
TOKEN_GROUP = 16
ATT_HEADS = 8
ATT_DH = 64
ATT_WIN = 512
MAX_REL = 128
N_BRANCH = 3
D_FF = 2816
EPS = 1e-6
MASKED = -1e30
LANES = 128

MIX_U, MIX_VB, MIX_RB, MIX_QC, MIX_KC, MIX_VC = 0, 512, 1024, 1536, 2048, 2560
MIX_W = 3072
GLA_Q, GLA_K, GLA_Z = 0, 256, 512
GLA_W = 640
GATE_W = N_BRANCH * D_MODEL
YS_A, YS_B, YS_C = 0, 512, 1024

ATT_STAGE_GAP = 1
MERGE_SPLIT = 2
ROW_TILE = 256
SEQ_TILE = 512
VMEM_LIMIT = 56 * 1024 * 1024

NT_DIMS = (((1,), (1,)), ((), ()))
TN_DIMS = (((0,), (0,)), ((), ()))


def _sigmoid(x):
    return 1.0 / (1.0 + jnp.exp(-x))


def _rms(x, g):
    return x * lax.rsqrt(jnp.mean(x * x, axis=-1, keepdims=True) + EPS) * g


def _layer_spec(arr, l):
    tail = (0,) * (arr.ndim - 1)
    return pl.BlockSpec((None,) + arr.shape[1:], lambda *_: (l,) + tail, pipeline_mode=pl.Buffered(1))


def _inproj_kernel(x_ref, g_ref, wg_ref, wm_ref, wl_ref, og_ref, om_ref, ol_ref):
    tm = x_ref.shape[0]
    for h in range(MERGE_SPLIT):
        rows = slice(h * (tm // MERGE_SPLIT), (h + 1) * (tm // MERGE_SPLIT))
        xb = _rms(x_ref[rows, :], g_ref[...]).astype(BF16)
        og_ref[rows, :] = jnp.dot(xb, wg_ref[...], preferred_element_type=F32)
        om_ref[rows, :] = jnp.dot(xb, wm_ref[...], preferred_element_type=F32)
        ol_ref[rows, :] = jnp.dot(xb, wl_ref[...], preferred_element_type=F32)


def _inproj(x, pw, l):
    rows = x.shape[0]
    tm = min(MERGE_SPLIT * ROW_TILE, rows)
    row_spec = lambda w: pl.BlockSpec((tm, w), lambda i: (i, 0))
    consts = [pw["attn_norm_g"], pw["w_gates"], pw["w_mix"], pw["w_gla"]]
    return pl.pallas_call(
        _inproj_kernel,
        grid=(rows // tm,),
        in_specs=[row_spec(D_MODEL)] + [_layer_spec(c, l) for c in consts],
        out_specs=[row_spec(GATE_W), row_spec(MIX_W), row_spec(GLA_W)],
        out_shape=[jax.ShapeDtypeStruct((rows, GATE_W), F32), jax.ShapeDtypeStruct((rows, MIX_W), F32),
                   jax.ShapeDtypeStruct((rows, GLA_W), F32)],
        compiler_params=pltpu.CompilerParams(dimension_semantics=("arbitrary",), vmem_limit_bytes=VMEM_LIMIT),
        name="in_proj",
    )(x, *consts)


def _mixer_kernel(tt, lg, lq, grp, has_state, pos0, *refs):
    refs = list(refs)
    mix_ref, gla_ref = refs[:2]
    k = 2
    if has_state:
        hist_ref, s0_ref, kc_ref, vc_ref = refs[k:k + 4]
        k += 4
    wg2_ref, bg_ref, gng_ref, pmap_ref, pscale_ref, bias_ref = refs[k:k + 6]
    k += 6
    ys_ref, sfin_ref, pool_ref, knew_ref, vnew_ref = refs[k:k + 5]
    carry_ref, st_ref, kbuf_ref, vbuf_ref, la_ref = refs[k + 5:]

    i = pl.program_id(1)
    lane = lax.broadcasted_iota(jnp.int32, (1, LANES), 1)
    head_mask = (lane < ATT_DH, lane >= ATT_DH)

    @pl.when(i == 0)
    def _init():
        if has_state:
            carry_ref[...] = hist_ref[0]
            st_ref[...] = s0_ref[0]
            kbuf_ref[0:ATT_WIN, :] = kc_ref[0].astype(BF16)
            vbuf_ref[0:ATT_WIN, :] = vc_ref[0].astype(BF16)
        else:
            carry_ref[...] = jnp.zeros_like(carry_ref)
            st_ref[...] = jnp.zeros_like(st_ref)
            kbuf_ref[0:ATT_WIN, :] = jnp.zeros((ATT_WIN, MIX), BF16)
            vbuf_ref[0:ATT_WIN, :] = jnp.zeros((ATT_WIN, MIX), BF16)

    u = mix_ref[:, MIX_U:MIX_U + MIX]
    ext = jnp.concatenate([carry_ref[...], u], axis=0)
    carry_ref[...] = u[tt - HIST_ROWS:, :]
    row = lax.broadcasted_iota(jnp.int32, (tt, POOL_GW), 0)
    n_seen = (pos0 + i * tt + row + 1).astype(F32)

    def pool_group(g, w):
        cols = slice(g * POOL_GW, (g + 1) * POOL_GW)
        s = ext[:, cols]
        shift = 1
        while shift < w:
            s = s + pltpu.roll(s, shift, 0)
            shift *= 2
        p = s[HIST_ROWS:, :] / jnp.minimum(float(w), n_seen) - u[:, cols]
        y = jnp.dot(p.astype(BF16), pmap_ref[g], preferred_element_type=F32) * pscale_ref[:, cols]
        ys_ref[:, YS_A + g * POOL_GW:YS_A + (g + 1) * POOL_GW] = y.astype(BF16)

    gla_in = {}
    row_in_blk = lax.broadcasted_iota(jnp.int32, (tt, GLA_QK_W), 0) & (lg - 1)

    def decay_logits():
        z = gla_ref[:, GLA_Z:GLA_Z + LANES].astype(BF16)
        pre = jnp.dot(z, wg2_ref[...], preferred_element_type=F32) + bg_ref[...]
        gla_in["log_a"] = -(jnp.maximum(-pre, 0.0) + jnp.log1p(jnp.exp(-jnp.abs(pre)))) / GLA_TAU
        gla_in["b"] = gla_in["log_a"]

    def scan_step(shift):
        b = gla_in["b"]
        gla_in["b"] = b + jnp.where(row_in_blk >= shift, pltpu.roll(b, shift, 0), 0.0)

    side_jobs = [functools.partial(pool_group, g, w) for g, w in enumerate(POOL_WINDOWS)]
    side_jobs += [decay_logits] + [functools.partial(scan_step, 1 << k) for k in range(lg.bit_length() - 1)]

    ri = lax.broadcasted_iota(jnp.int32, (2 * lg, lg), 0) & (lg - 1)
    ci = lax.broadcasted_iota(jnp.int32, (2 * lg, lg), 1)
    causal = ci <= ri

    def stack_heads(x):
        return jnp.concatenate([jnp.where(head_mask[0], x, 0.0), jnp.where(head_mask[1], x, 0.0)],
                               axis=0).astype(BF16)

    def gla_out(rows, h, o):
        r = mix_ref[rows, MIX_RB + h * GLA_DV:MIX_RB + (h + 1) * GLA_DV]
        y = _rms(o, gng_ref[...]) * (r * _sigmoid(r))
        ys_ref[rows, YS_B + h * GLA_DV:YS_B + (h + 1) * GLA_DV] = y.astype(BF16)

    kbuf_ref[ATT_WIN:ATT_WIN + tt, :] = mix_ref[:, MIX_KC:MIX_KC + MIX].astype(BF16)
    vbuf_ref[ATT_WIN:ATT_WIN + tt, :] = mix_ref[:, MIX_VC:MIX_VC + MIX].astype(BF16)
    gq = grp * lq
    win = ATT_WIN + gq
    kcol = lax.broadcasted_iota(jnp.int32, (1, win), 1)
    first_tile_neg = jnp.where(i == 0, MASKED, 0.0).astype(F32)

    def scores(gi, p):
        q2 = mix_ref[gi * gq:(gi + 1) * gq, MIX_QC + p * LANES:MIX_QC + (p + 1) * LANES] * (ATT_DH ** -0.5)
        s = lax.dot_general(stack_heads(q2), kbuf_ref[gi * gq:gi * gq + win, p * LANES:(p + 1) * LANES],
                            NT_DIMS, preferred_element_type=F32)
        s = s + bias_ref[p]
        if not has_state:
            s = s + jnp.where(kcol < ATT_WIN - gi * gq, first_tile_neg, 0.0)
        return s

    def weights(s):
        e = jnp.exp(s - jnp.max(s, axis=-1, keepdims=True))
        return e.astype(BF16), jnp.sum(e, axis=-1, keepdims=True)

    def attend(gi, p, e, den):
        o = jnp.dot(e, vbuf_ref[gi * gq:gi * gq + win, p * LANES:(p + 1) * LANES],
                    preferred_element_type=F32) / den
        ys_ref[gi * gq:(gi + 1) * gq, YS_C + p * LANES:YS_C + (p + 1) * LANES] = (
            jnp.where(head_mask[0], o[:gq], o[gq:]).astype(BF16))

    units = [(gi, p) for gi in range(tt // gq) for p in range(ATT_HEADS // 2)]
    s_q, w_q = {}, {}
    for t in range(max(len(units) + 2 * ATT_STAGE_GAP, len(side_jobs))):
        if t < len(side_jobs):
            side_jobs[t]()
        if t < len(units):
            s_q[t] = scores(*units[t])
        if ATT_STAGE_GAP <= t < len(units) + ATT_STAGE_GAP:
            w_q[t - ATT_STAGE_GAP] = weights(s_q.pop(t - ATT_STAGE_GAP))
        if 2 * ATT_STAGE_GAP <= t < len(units) + 2 * ATT_STAGE_GAP:
            attend(*units[t - 2 * ATT_STAGE_GAP], *w_q.pop(t - 2 * ATT_STAGE_GAP))

    if not has_state:
        kbuf_ref[0:ATT_WIN, :] = kbuf_ref[tt:tt + ATT_WIN, :]
        vbuf_ref[0:ATT_WIN, :] = vbuf_ref[tt:tt + ATT_WIN, :]

    log_a, b = gla_in["log_a"], gla_in["b"]
    blocks_safe = jnp.max(-b) <= GLA_SAFE_DECAY

    @pl.when(blocks_safe)
    def _gla_blocks():
        n_pair, n_blk = GLA_HEADS // 2, tt // lg
        blk = lambda c: slice(c * lg, (c + 1) * lg)
        eb, qs, kt = [], [], []
        for p in range(n_pair):
            bq = b[:, p * LANES:(p + 1) * LANES]
            eb.append(jnp.exp(bq))
            qs.append(gla_ref[:, GLA_Q + p * LANES:GLA_Q + (p + 1) * LANES] * (GLA_DK ** -0.5) * eb[p])
            kt.append(gla_ref[:, GLA_K + p * LANES:GLA_K + (p + 1) * LANES] * jnp.exp(-bq))

        def scores_and_increment(p, c):
            e_last = eb[p][(c + 1) * lg - 1:(c + 1) * lg, :]
            vb = [mix_ref[blk(c), MIX_VB + (2 * p + hl) * GLA_DV:MIX_VB + (2 * p + hl + 1) * GLA_DV].astype(BF16)
                  for hl in range(2)]
            qst = stack_heads(qs[p][blk(c)])
            a = lax.dot_general(qst, kt[p][blk(c)].astype(BF16), NT_DIMS, preferred_element_type=F32)
            att = jnp.where(causal, a, 0.0).astype(BF16)
            kdb = (kt[p][blk(c)] * e_last).astype(BF16)
            u2 = [lax.dot_general(vb[hl], kdb, TN_DIMS, preferred_element_type=F32) for hl in range(2)]
            return e_last, vb, qst, att, jnp.where(head_mask[0], u2[0], u2[1])

        def read_out(p, c, vb, qst, att, start):
            inter = lax.dot_general(qst, start.astype(BF16), NT_DIMS, preferred_element_type=F32)
            for hl in range(2):
                half = slice(hl * lg, (hl + 1) * lg)
                gla_out(blk(c), 2 * p + hl, jnp.dot(att[half], vb[hl], preferred_element_type=F32) + inter[half])

        state = [st_ref[p] for p in range(n_pair)]
        pending = None
        for c in range(n_blk):
            for p in range(n_pair):
                e_last, vb, qst, att, upd = scores_and_increment(p, c)
                start = state[p]
                state[p] = e_last * start + upd
                if pending is not None:
                    read_out(*pending)
                pending = (p, c, vb, qst, att, start)
        read_out(*pending)
        for p in range(n_pair):
            st_ref[p] = state[p]

    @pl.when(jnp.logical_not(blocks_safe))
    def _gla_tokens():
        la_ref[...] = log_a
        eye = (lax.broadcasted_iota(jnp.int32, (LANES, LANES), 0)
               == lax.broadcasted_iota(jnp.int32, (LANES, LANES), 1)).astype(F32)
        tok = lax.broadcasted_iota(jnp.int32, (TOKEN_GROUP, LANES), 0)
        for p in range(GLA_HEADS // 2):
            def group(gidx, st, p=p):
                rows = pl.ds(pl.multiple_of(gidx * TOKEN_GROUP, TOKEN_GROUP), TOKEN_GROUP)
                a = jnp.exp(la_ref[rows, p * LANES:(p + 1) * LANES])
                q = gla_ref[rows, GLA_Q + p * LANES:GLA_Q + (p + 1) * LANES] * (GLA_DK ** -0.5)
                k = gla_ref[rows, GLA_K + p * LANES:GLA_K + (p + 1) * LANES]
                v_t = [lax.dot_general(eye, mix_ref[rows, MIX_VB + (2 * p + hl) * GLA_DV:
                                                    MIX_VB + (2 * p + hl + 1) * GLA_DV],
                                       NT_DIMS, precision=lax.Precision.HIGHEST, preferred_element_type=F32)
                       for hl in range(2)]
                o = [jnp.zeros((TOKEN_GROUP, GLA_DV), F32), jnp.zeros((TOKEN_GROUP, GLA_DV), F32)]
                for t in range(TOKEN_GROUP):
                    v_col = jnp.where(head_mask[0], v_t[0][:, t:t + 1], v_t[1][:, t:t + 1])
                    st = a[t:t + 1, :] * st + v_col * k[t:t + 1, :]
                    stb = st.astype(BF16)
                    for hl in range(2):
                        q_t = jnp.where(tok == t, jnp.where(head_mask[hl], q, 0.0), 0.0).astype(BF16)
                        o[hl] = o[hl] + lax.dot_general(q_t, stb, NT_DIMS, preferred_element_type=F32)
                for hl in range(2):
                    gla_out(rows, 2 * p + hl, o[hl])
                return st

            st_ref[p] = lax.fori_loop(0, tt // TOKEN_GROUP, group, st_ref[p])

    @pl.when(i == pl.num_programs(1) - 1)
    def _fin():
        keep = min(ATT_WIN, tt)
        sfin_ref[0] = st_ref[...]
        pool_ref[0] = mix_ref[tt - HIST_ROWS:, MIX_U:MIX_U + MIX]
        knew_ref[0] = mix_ref[tt - keep:, MIX_KC:MIX_KC + MIX]
        vnew_ref[0] = mix_ref[tt - keep:, MIX_VC:MIX_VC + MIX]


def _mixers(mix, gla, state, pw, l, n_seq, seq_len, pos0):
    has_state = state is not None
    tt = min(SEQ_TILE, seq_len)
    lg = min(CHUNK, seq_len)
    lq = min(CHUNK, seq_len)
    grp = _att_group(seq_len)
    nt = seq_len // tt
    assert has_state or tt == ATT_WIN
    kernel = functools.partial(_mixer_kernel, tt, lg, lq, grp, has_state, pos0)

    tile = lambda w: pl.BlockSpec((tt, w), lambda s, i: (s * nt + i, 0))
    in_specs = [tile(MIX_W), tile(GLA_W)]
    args = [mix, gla]
    if has_state:
        hist, s0t, kc, vc = state
        in_specs += [pl.BlockSpec((1, HIST_ROWS, MIX), lambda s, i: (s, 0, 0)),
                     pl.BlockSpec((1, 2, LANES, LANES), lambda s, i: (s, 0, 0, 0)),
                     pl.BlockSpec((1, ATT_WIN, MIX), lambda s, i: (s, 0, 0)),
                     pl.BlockSpec((1, ATT_WIN, MIX), lambda s, i: (s, 0, 0))]
        args += [hist, s0t, kc, vc]
    consts = [pw["w_gate2"], pw["b_gate"], pw["gla_norm_g"], pw["pool_map"], pw["pool_scale"],
              pw["bias_s"] if has_state else pw["bias_p"]]
    in_specs += [_layer_spec(c, l) for c in consts]
    args += consts
    keep = min(ATT_WIN, seq_len)
    per_seq = lambda r: pl.BlockSpec((1, r, MIX), lambda s, i: (s, 0, 0))

    return pl.pallas_call(
        kernel,
        grid=(n_seq, nt),
        in_specs=in_specs,
        out_specs=[pl.BlockSpec((tt, 3 * MIX), lambda s, i: (s * nt + i, 0)),
                   pl.BlockSpec((1, 2, LANES, LANES), lambda s, i: (s, 0, 0, 0)),
                   per_seq(HIST_ROWS), per_seq(keep), per_seq(keep)],
        out_shape=[jax.ShapeDtypeStruct((n_seq * seq_len, 3 * MIX), BF16),
                   jax.ShapeDtypeStruct((n_seq, 2, LANES, LANES), F32),
                   jax.ShapeDtypeStruct((n_seq, HIST_ROWS, MIX), F32),
                   jax.ShapeDtypeStruct((n_seq, keep, MIX), F32),
                   jax.ShapeDtypeStruct((n_seq, keep, MIX), F32)],
        scratch_shapes=[pltpu.VMEM((HIST_ROWS, MIX), F32),
                        pltpu.VMEM((2, LANES, LANES), F32),
                        pltpu.VMEM((ATT_WIN + tt, MIX), BF16),
                        pltpu.VMEM((ATT_WIN + tt, MIX), BF16),
                        pltpu.VMEM((tt, GLA_QK_W), F32)],
        compiler_params=pltpu.CompilerParams(dimension_semantics=("arbitrary", "arbitrary"),
                                             vmem_limit_bytes=VMEM_LIMIT),
        name="mixers_sample" if has_state else "mixers_prompt",
    )(*args)


def _merge_ffn_kernel(final, x_ref, ys_ref, hg_ref, wbr_ref, wout_ref, g2_ref, wf_ref, wo_ref, fg_ref, out_ref):
    tm = x_ref.shape[0]
    halves = [slice(h * (tm // MERGE_SPLIT), (h + 1) * (tm // MERGE_SPLIT)) for h in range(MERGE_SPLIT)]

    def merge(rows):
        merged = None
        for g in range(N_BRANCH):
            branch = jnp.dot(ys_ref[rows, g * MIX:(g + 1) * MIX], wbr_ref[g], preferred_element_type=F32)
            term = _sigmoid(hg_ref[rows, g * D_MODEL:(g + 1) * D_MODEL]) * branch
            merged = term if merged is None else merged + term
        return merged.astype(BF16)

    def residual(rows, merged):
        x1 = x_ref[rows, :] + jnp.dot(merged, wout_ref[...], preferred_element_type=F32)
        return x1, _rms(x1, g2_ref[...]).astype(BF16)

    def hidden(xn):
        a = jnp.dot(xn, wf_ref[:, :D_FF], preferred_element_type=F32)
        gt = jnp.dot(xn, wf_ref[:, D_FF:], preferred_element_type=F32)
        return (a * _sigmoid(a) * gt).astype(BF16)

    def finish(rows, x1, act):
        x2 = x1 + jnp.dot(act, wo_ref[...], preferred_element_type=F32)
        if final:
            x2 = _rms(x2, fg_ref[...])
        out_ref[rows, :] = x2

    merged = [merge(r) for r in halves]
    res = [residual(r, m) for r, m in zip(halves, merged)]
    acts = [hidden(xn) for _, xn in res]
    for r, (x1, _), act in zip(halves, res, acts):
        finish(r, x1, act)


def _merge_ffn(x, ys, gates, pw, l, final):
    rows = x.shape[0]
    tm = min(MERGE_SPLIT * ROW_TILE, rows)
    row_spec = lambda w: pl.BlockSpec((tm, w), lambda i: (i, 0))
    consts = [pw["w_branch"], pw["w_out"], pw["ffn_norm_g"], pw["w_ffn_in"], pw["w_ffn_out"]]
    return pl.pallas_call(
        functools.partial(_merge_ffn_kernel, final),
        grid=(rows // tm,),
        in_specs=([row_spec(D_MODEL), row_spec(3 * MIX), row_spec(GATE_W)] + [_layer_spec(c, l) for c in consts]
                  + [_layer_spec(pw["final_norm_g"], 0)]),
        out_specs=row_spec(D_MODEL),
        out_shape=jax.ShapeDtypeStruct((rows, D_MODEL), F32),
        compiler_params=pltpu.CompilerParams(dimension_semantics=("arbitrary",), vmem_limit_bytes=VMEM_LIMIT),
        name="merge_ffn",
    )(x, ys, gates, *consts, pw["final_norm_g"])


def _att_group(seq_len):
    return 2 if seq_len >= 2 * CHUNK else 1


def _band_bias(rel_bias, lq, grp):
    assert lq - 1 <= MAX_REL
    depth = rel_bias.shape[0]
    rtab = rel_bias[..., ::-1].astype(F32)
    n_far = ATT_WIN + lq - MAX_REL
    ext = jnp.concatenate([jnp.broadcast_to(rtab[..., :1], (depth, ATT_HEADS, n_far)),
                           rtab[..., 1:MAX_REL + lq]], axis=-1)
    wk = ATT_WIN + lq
    period = wk + lq - 1
    rot = jnp.concatenate([ext[..., lq - 1:], ext[..., :lq - 1]], axis=-1)
    chunk = jnp.tile(rot, (1, 1, lq))[..., :lq * (period - 1)].reshape(depth, ATT_HEADS, lq, period - 1)[..., :wk]
    blocks = [jnp.pad(chunk, ((0, 0), (0, 0), (0, 0), (j * lq, (grp - 1 - j) * lq)), constant_values=MASKED)
              for j in range(grp)]
    per_head = jnp.concatenate(blocks, axis=2)
    return per_head.reshape(depth, ATT_HEADS // 2, 2 * grp * lq, ATT_WIN + grp * lq)


IN_COLS = (0, 512, 768, 1024, 1536, 2048, 2064, 2576, 3088, 3600, 6672)


def _regroup_kernel(w_ref, og_ref, om_ref, ol_ref):
    w = w_ref[...]
    grp = lambda a: w[:, IN_COLS[a]:IN_COLS[a + 1]].astype(BF16)
    og_ref[...] = grp(9)
    for k, a in enumerate((0, 3, 4, 6, 7, 8)):
        om_ref[:, k * MIX:(k + 1) * MIX] = grp(a)
    ol_ref[:, GLA_Q:GLA_Q + GLA_QK_W] = grp(1)
    ol_ref[:, GLA_K:GLA_K + GLA_QK_W] = grp(2)
    ol_ref[:, GLA_Z:GLA_Z + LANES] = jnp.concatenate(
        [grp(5), jnp.zeros((w.shape[0], LANES - GLA_RANK), BF16)], axis=1)


def _regroup_in_proj(w_in):
    depth, rows, cols = w_in.shape
    tr = ROW_TILE
    out = lambda w: pl.BlockSpec((None, tr, w), lambda l, i: (l, i, 0))
    return pl.pallas_call(
        _regroup_kernel,
        grid=(depth, rows // tr),
        in_specs=[pl.BlockSpec((None, tr, cols), lambda l, i: (l, i, 0))],
        out_specs=[out(GATE_W), out(MIX_W), out(GLA_W)],
        out_shape=[jax.ShapeDtypeStruct((depth, rows, GATE_W), BF16), jax.ShapeDtypeStruct((depth, rows, MIX_W), BF16),
                   jax.ShapeDtypeStruct((depth, rows, GLA_W), BF16)],
        compiler_params=pltpu.CompilerParams(dimension_semantics=("arbitrary", "arbitrary"),
                                             vmem_limit_bytes=VMEM_LIMIT),
        name="regroup_in_proj",
    )(w_in)


def _prep_params(attn_norm_g, w_in, w_gate2, b_gate, gla_norm_g, pool_map, pool_scale, rel_bias, w_branch, w_out,
                 ffn_norm_g, w_ffn_in, w_ffn_out, final_norm_g, prompt_len, sample_len):
    depth = w_in.shape[0]
    w_gates, w_mix, w_gla = _regroup_in_proj(w_in)
    wg2 = jnp.concatenate([w_gate2, jnp.zeros((depth, LANES - GLA_RANK, GLA_QK_W), w_gate2.dtype)], axis=1)
    vec = lambda a: a[:, None, :]
    return dict(
        attn_norm_g=vec(attn_norm_g), w_gates=w_gates, w_mix=w_mix,
        w_gla=w_gla, w_gate2=wg2.astype(BF16), b_gate=vec(b_gate),
        gla_norm_g=vec(gla_norm_g), pool_map=pool_map.astype(BF16), pool_scale=vec(pool_scale),
        bias_p=_band_bias(rel_bias, CHUNK, _att_group(prompt_len)),
        bias_s=_band_bias(rel_bias, min(CHUNK, sample_len), _att_group(sample_len)),
        w_branch=w_branch.astype(BF16), w_out=w_out.astype(BF16), ffn_norm_g=vec(ffn_norm_g),
        w_ffn_in=w_ffn_in.astype(BF16), w_ffn_out=w_ffn_out.astype(BF16),
        final_norm_g=final_norm_g[None, None, :])


def _layer(x, n_seq, seq_len, state, pw, l, pos0, final):
    gates, mix, gla = _inproj(x, pw, l)
    if state is not None:
        hist, s0, kc, vc = state
        hist = jnp.pad(hist, ((0, 0), (HIST_ROWS - POOL_HIST, 0), (0, 0)))
        s0t = s0.reshape(n_seq, 2, 2, GLA_DK, GLA_DV).transpose(0, 1, 4, 2, 3).reshape(n_seq, 2, LANES, LANES)
        state = (hist, s0t, kc.reshape(n_seq, ATT_WIN, MIX), vc.reshape(n_seq, ATT_WIN, MIX))
    ys, s_fin_t, pool16, k_new, v_new = _mixers(mix, gla, state, pw, l, n_seq, seq_len, pos0)
    x_new = _merge_ffn(x, ys, gates, pw, l, final)

    keep = min(ATT_WIN, seq_len)
    gla_new = (s_fin_t.reshape(n_seq, 2, GLA_DV, 2, GLA_DK).transpose(0, 1, 3, 4, 2)
               .reshape(n_seq, GLA_HEADS, GLA_DK, GLA_DV))
    return x_new, (pool16[:, HIST_ROWS - POOL_HIST:], gla_new,
                   k_new.reshape(n_seq, keep, ATT_HEADS, ATT_DH), v_new.reshape(n_seq, keep, ATT_HEADS, ATT_DH))


def kernel(x_prompt, x_sample, cache_pool, state_gla, cache_k, cache_v, attn_norm_g, w_in, w_gate2, b_gate,
           gla_norm_g, pool_map, pool_scale, rel_bias, w_branch, w_out, ffn_norm_g, w_ffn_in, w_ffn_out,
           final_norm_g):
    batch, seq, _ = x_prompt.shape
    dec_batch, dec_seq, _ = x_sample.shape
    depth = w_in.shape[0]
    hp = x_prompt.reshape(batch * seq, D_MODEL)
    hs = x_sample.reshape(dec_batch * dec_seq, D_MODEL)
    pw = _prep_params(attn_norm_g, w_in, w_gate2, b_gate, gla_norm_g, pool_map, pool_scale, rel_bias, w_branch,
                      w_out, ffn_norm_g, w_ffn_in, w_ffn_out, final_norm_g, seq, dec_seq)
    outs_p, outs_s = [], []
    for l in range(depth):
        final = l == depth - 1
        hp, sp = _layer(hp, batch, seq, None, pw, l, 0, final)
        hs, ss = _layer(hs, dec_batch, dec_seq, (cache_pool[l], state_gla[l], cache_k[l], cache_v[l]), pw, l,
                        PAST_LEN, final)
        outs_p.append(sp)
        outs_s.append(ss)
    stack = lambda outs, j: jnp.stack([o[j] for o in outs])
    return (hp.reshape(batch, seq, D_MODEL), hs.reshape(dec_batch, dec_seq, D_MODEL),
            stack(outs_p, 0), stack(outs_s, 0), stack(outs_p, 1), stack(outs_s, 1),
            stack(outs_p, 2), stack(outs_s, 2), stack(outs_p, 3), stack(outs_s, 3))
```

```python
import functools

import jax
import jax.numpy as jnp
from jax import lax
from jax.experimental import pallas as pl
from jax.experimental.pallas import tpu as pltpu

F32 = jnp.float32
BF16 = jnp.bfloat16

D_MODEL = 1024
PAST_LEN = 2048
CHUNK = 64
MIX = 512
POOL_WINDOWS = (2, 4, 8, 16)
POOL_GW = 128
POOL_HIST = 15
HIST_ROWS = 16
GLA_HEADS = 4
GLA_DK = 64
GLA_DV = 128
GLA_QK_W = 256
GLA_RANK = 16
GLA_TAU = 16.0
GLA_SAFE_DECAY = 60.0
TOKEN_GROUP = 16
ATT_HEADS = 8
ATT_DH = 64
ATT_WIN = 512
MAX_REL = 128
N_BRANCH = 3
D_FF = 2816
EPS = 1e-6
MASKED = -1e30
LANES = 128

MIX_U, MIX_VB, MIX_RB, MIX_QC, MIX_KC, MIX_VC = 0, 512, 1024, 1536, 2048, 2560
MIX_W = 3072
GLA_Q, GLA_K, GLA_Z = 0, 256, 512
GLA_W = 640
GATE_W = N_BRANCH * D_MODEL
YS_A, YS_B, YS_C = 0, 512, 1024

ATT_STAGE_GAP = 1
MERGE_SPLIT = 2
ROW_TILE = 256
SEQ_TILE = 512
VMEM_LIMIT = 56 * 1024 * 1024

NT_DIMS = (((1,), (1,)), ((), ()))
TN_DIMS = (((0,), (0,)), ((), ()))


def _sigmoid(x):
    return 1.0 / (1.0 + jnp.exp(-x))


def _rms(x, g):
    return x * lax.rsqrt(jnp.mean(x * x, axis=-1, keepdims=True) + EPS) * g


def _layer_spec(arr, l):
    tail = (0,) * (arr.ndim - 1)
    return pl.BlockSpec((None,) + arr.shape[1:], lambda *_: (l,) + tail, pipeline_mode=pl.Buffered(1))


def _inproj_kernel(x_ref, g_ref, wg_ref, wm_ref, wl_ref, og_ref, om_ref, ol_ref):
    tm = x_ref.shape[0]
    for h in range(MERGE_SPLIT):
        rows = slice(h * (tm // MERGE_SPLIT), (h + 1) * (tm // MERGE_SPLIT))
        xb = _rms(x_ref[rows, :], g_ref[...]).astype(BF16)
        og_ref[rows, :] = jnp.dot(xb, wg_ref[...], preferred_element_type=F32)
        om_ref[rows, :] = jnp.dot(xb, wm_ref[...], preferred_element_type=F32)
        ol_ref[rows, :] = jnp.dot(xb, wl_ref[...], preferred_element_type=F32)


def _inproj(x, pw, l):
    rows = x.shape[0]
    tm = min(MERGE_SPLIT * ROW_TILE, rows)
    row_spec = lambda w: pl.BlockSpec((tm, w), lambda i: (i, 0))
    consts = [pw["attn_norm_g"], pw["w_gates"], pw["w_mix"], pw["w_gla"]]
    return pl.pallas_call(
        _inproj_kernel,
        grid=(rows // tm,),
        in_specs=[row_spec(D_MODEL)] + [_layer_spec(c, l) for c in consts],
        out_specs=[row_spec(GATE_W), row_spec(MIX_W), row_spec(GLA_W)],
        out_shape=[jax.ShapeDtypeStruct((rows, GATE_W), F32), jax.ShapeDtypeStruct((rows, MIX_W), F32),
                   jax.ShapeDtypeStruct((rows, GLA_W), F32)],
        compiler_params=pltpu.CompilerParams(dimension_semantics=("arbitrary",), vmem_limit_bytes=VMEM_LIMIT),
        name="in_proj",
    )(x, *consts)


def _mixer_kernel(tt, lg, lq, grp, has_state, pos0, *refs):
    refs = list(refs)
    mix_ref, gla_ref = refs[:2]
    k = 2
    if has_state:
        hist_ref, s0_ref, kc_ref, vc_ref = refs[k:k + 4]
        k += 4
    wg2_ref, bg_ref, gng_ref, pmap_ref, pscale_ref, bias_ref = refs[k:k + 6]
    k += 6
    ys_ref, sfin_ref, pool_ref, knew_ref, vnew_ref = refs[k:k + 5]
    carry_ref, st_ref, kbuf_ref, vbuf_ref, la_ref = refs[k + 5:]

    i = pl.program_id(1)
    lane = lax.broadcasted_iota(jnp.int32, (1, LANES), 1)
    head_mask = (lane < ATT_DH, lane >= ATT_DH)

    @pl.when(i == 0)
    def _init():
        if has_state:
            carry_ref[...] = hist_ref[0]
            st_ref[...] = s0_ref[0]
            kbuf_ref[0:ATT_WIN, :] = kc_ref[0].astype(BF16)
            vbuf_ref[0:ATT_WIN, :] = vc_ref[0].astype(BF16)
        else:
            carry_ref[...] = jnp.zeros_like(carry_ref)
            st_ref[...] = jnp.zeros_like(st_ref)
            kbuf_ref[0:ATT_WIN, :] = jnp.zeros((ATT_WIN, MIX), BF16)
            vbuf_ref[0:ATT_WIN, :] = jnp.zeros((ATT_WIN, MIX), BF16)

    u = mix_ref[:, MIX_U:MIX_U + MIX]
    ext = jnp.concatenate([carry_ref[...], u], axis=0)
    carry_ref[...] = u[tt - HIST_ROWS:, :]
    row = lax.broadcasted_iota(jnp.int32, (tt, POOL_GW), 0)
    n_seen = (pos0 + i * tt + row + 1).astype(F32)

    def pool_group(g, w):
        cols = slice(g * POOL_GW, (g + 1) * POOL_GW)
        s = ext[:, cols]
        shift = 1
        while shift < w:
            s = s + pltpu.roll(s, shift, 0)
            shift *= 2
        p = s[HIST_ROWS:, :] / jnp.minimum(float(w), n_seen) - u[:, cols]
        y = jnp.dot(p.astype(BF16), pmap_ref[g], preferred_element_type=F32) * pscale_ref[:, cols]
        ys_ref[:, YS_A + g * POOL_GW:YS_A + (g + 1) * POOL_GW] = y.astype(BF16)

    gla_in = {}
    row_in_blk = lax.broadcasted_iota(jnp.int32, (tt, GLA_QK_W), 0) & (lg - 1)

    def decay_logits():
        z = gla_ref[:, GLA_Z:GLA_Z + LANES].astype(BF16)
        pre = jnp.dot(z, wg2_ref[...], preferred_element_type=F32) + bg_ref[...]
        gla_in["log_a"] = -(jnp.maximum(-pre, 0.0) + jnp.log1p(jnp.exp(-jnp.abs(pre)))) / GLA_TAU
        gla_in["b"] = gla_in["log_a"]

    def scan_step(shift):
        b = gla_in["b"]
        gla_in["b"] = b + jnp.where(row_in_blk >= shift, pltpu.roll(b, shift, 0), 0.0)

    side_jobs = [functools.partial(pool_group, g, w) for g, w in enumerate(POOL_WINDOWS)]
    side_jobs += [decay_logits] + [functools.partial(scan_step, 1 << k) for k in range(lg.bit_length() - 1)]

    ri = lax.broadcasted_iota(jnp.int32, (2 * lg, lg), 0) & (lg - 1)
    ci = lax.broadcasted_iota(jnp.int32, (2 * lg, lg), 1)
    causal = ci <= ri

    def stack_heads(x):
        return jnp.concatenate([jnp.where(head_mask[0], x, 0.0), jnp.where(head_mask[1], x, 0.0)],
                               axis=0).astype(BF16)

    def gla_out(rows, h, o):
        r = mix_ref[rows, MIX_RB + h * GLA_DV:MIX_RB + (h + 1) * GLA_DV]
        y = _rms(o, gng_ref[...]) * (r * _sigmoid(r))
        ys_ref[rows, YS_B + h * GLA_DV:YS_B + (h + 1) * GLA_DV] = y.astype(BF16)

    kbuf_ref[ATT_WIN:ATT_WIN + tt, :] = mix_ref[:, MIX_KC:MIX_KC + MIX].astype(BF16)
    vbuf_ref[ATT_WIN:ATT_WIN + tt, :] = mix_ref[:, MIX_VC:MIX_VC + MIX].astype(BF16)
    gq = grp * lq
    win = ATT_WIN + gq
    kcol = lax.broadcasted_iota(jnp.int32, (1, win), 1)
    first_tile_neg = jnp.where(i == 0, MASKED, 0.0).astype(F32)

    def scores(gi, p):
        q2 = mix_ref[gi * gq:(gi + 1) * gq, MIX_QC + p * LANES:MIX_QC + (p + 1) * LANES] * (ATT_DH ** -0.5)
        s = lax.dot_general(stack_heads(q2), kbuf_ref[gi * gq:gi * gq + win, p * LANES:(p + 1) * LANES],
                            NT_DIMS, preferred_element_type=F32)
        s = s + bias_ref[p]
        if not has_state:
            s = s + jnp.where(kcol < ATT_WIN - gi * gq, first_tile_neg, 0.0)
        return s

    def weights(s):
        e = jnp.exp(s - jnp.max(s, axis=-1, keepdims=True))
        return e.astype(BF16), jnp.sum(e, axis=-1, keepdims=True)

    def attend(gi, p, e, den):
        o = jnp.dot(e, vbuf_ref[gi * gq:gi * gq + win, p * LANES:(p + 1) * LANES],
                    preferred_element_type=F32) / den
        ys_ref[gi * gq:(gi + 1) * gq, YS_C + p * LANES:YS_C + (p + 1) * LANES] = (
            jnp.where(head_mask[0], o[:gq], o[gq:]).astype(BF16))

    units = [(gi, p) for gi in range(tt // gq) for p in range(ATT_HEADS // 2)]
    s_q, w_q = {}, {}
    for t in range(max(len(units) + 2 * ATT_STAGE_GAP, len(side_jobs))):
        if t < len(side_jobs):
            side_jobs[t]()
        if t < len(units):
            s_q[t] = scores(*units[t])
        if ATT_STAGE_GAP <= t < len(units) + ATT_STAGE_GAP:
            w_q[t - ATT_STAGE_GAP] = weights(s_q.pop(t - ATT_STAGE_GAP))
        if 2 * ATT_STAGE_GAP <= t < len(units) + 2 * ATT_STAGE_GAP:
            attend(*units[t - 2 * ATT_STAGE_GAP], *w_q.pop(t - 2 * ATT_STAGE_GAP))

    if not has_state:
        kbuf_ref[0:ATT_WIN, :] = kbuf_ref[tt:tt + ATT_WIN, :]
        vbuf_ref[0:ATT_WIN, :] = vbuf_ref[tt:tt + ATT_WIN, :]

    log_a, b = gla_in["log_a"], gla_in["b"]
    blocks_safe = jnp.max(-b) <= GLA_SAFE_DECAY

    @pl.when(blocks_safe)
    def _gla_blocks():
        n_pair, n_blk = GLA_HEADS // 2, tt // lg
        blk = lambda c: slice(c * lg, (c + 1) * lg)
        eb, qs, kt = [], [], []
        for p in range(n_pair):
            bq = b[:, p * LANES:(p + 1) * LANES]
            eb.append(jnp.exp(bq))
            qs.append(gla_ref[:, GLA_Q + p * LANES:GLA_Q + (p + 1) * LANES] * (GLA_DK ** -0.5) * eb[p])
            kt.append(gla_ref[:, GLA_K + p * LANES:GLA_K + (p + 1) * LANES] * jnp.exp(-bq))

        def scores_and_increment(p, c):
            e_last = eb[p][(c + 1) * lg - 1:(c + 1) * lg, :]
            vb = [mix_ref[blk(c), MIX_VB + (2 * p + hl) * GLA_DV:MIX_VB + (2 * p + hl + 1) * GLA_DV].astype(BF16)
                  for hl in range(2)]
            qst = stack_heads(qs[p][blk(c)])
            a = lax.dot_general(qst, kt[p][blk(c)].astype(BF16), NT_DIMS, preferred_element_type=F32)
            att = jnp.where(causal, a, 0.0).astype(BF16)
            kdb = (kt[p][blk(c)] * e_last).astype(BF16)
            u2 = [lax.dot_general(vb[hl], kdb, TN_DIMS, preferred_element_type=F32) for hl in range(2)]
            return e_last, vb, qst, att, jnp.where(head_mask[0], u2[0], u2[1])

        def read_out(p, c, vb, qst, att, start):
            inter = lax.dot_general(qst, start.astype(BF16), NT_DIMS, preferred_element_type=F32)
            for hl in range(2):
                half = slice(hl * lg, (hl + 1) * lg)
                gla_out(blk(c), 2 * p + hl, jnp.dot(att[half], vb[hl], preferred_element_type=F32) + inter[half])

        state = [st_ref[p] for p in range(n_pair)]
        pending = None
        for c in range(n_blk):
            for p in range(n_pair):
                e_last, vb, qst, att, upd = scores_and_increment(p, c)
                start = state[p]
                state[p] = e_last * start + upd
                if pending is not None:
                    read_out(*pending)
                pending = (p, c, vb, qst, att, start)
        read_out(*pending)
        for p in range(n_pair):
            st_ref[p] = state[p]

    @pl.when(jnp.logical_not(blocks_safe))
    def _gla_tokens():
        la_ref[...] = log_a
        eye = (lax.broadcasted_iota(jnp.int32, (LANES, LANES), 0)
               == lax.broadcasted_iota(jnp.int32, (LANES, LANES), 1)).astype(F32)
        tok = lax.broadcasted_iota(jnp.int32, (TOKEN_GROUP, LANES), 0)
        for p in range(GLA_HEADS // 2):
            def group(gidx, st, p=p):
                rows = pl.ds(pl.multiple_of(gidx * TOKEN_GROUP, TOKEN_GROUP), TOKEN_GROUP)
                a = jnp.exp(la_ref[rows, p * LANES:(p + 1) * LANES])
                q = gla_ref[rows, GLA_Q + p * LANES:GLA_Q + (p + 1) * LANES] * (GLA_DK ** -0.5)
                k = gla_ref[rows, GLA_K + p * LANES:GLA_K + (p + 1) * LANES]
                v_t = [lax.dot_general(eye, mix_ref[rows, MIX_VB + (2 * p + hl) * GLA_DV:
                                                    MIX_VB + (2 * p + hl + 1) * GLA_DV],
                                       NT_DIMS, precision=lax.Precision.HIGHEST, preferred_element_type=F32)
                       for hl in range(2)]
                o = [jnp.zeros((TOKEN_GROUP, GLA_DV), F32), jnp.zeros((TOKEN_GROUP, GLA_DV), F32)]
                for t in range(TOKEN_GROUP):
                    v_col = jnp.where(head_mask[0], v_t[0][:, t:t + 1], v_t[1][:, t:t + 1])
                    st = a[t:t + 1, :] * st + v_col * k[t:t + 1, :]
                    stb = st.astype(BF16)
                    for hl in range(2):
                        q_t = jnp.where(tok == t, jnp.where(head_mask[hl], q, 0.0), 0.0).astype(BF16)
                        o[hl] = o[hl] + lax.dot_general(q_t, stb, NT_DIMS, preferred_element_type=F32)
                for hl in range(2):
                    gla_out(rows, 2 * p + hl, o[hl])
                return st

            st_ref[p] = lax.fori_loop(0, tt // TOKEN_GROUP, group, st_ref[p])

    @pl.when(i == pl.num_programs(1) - 1)
    def _fin():
        keep = min(ATT_WIN, tt)
        sfin_ref[0] = st_ref[...]
        pool_ref[0] = mix_ref[tt - HIST_ROWS:, MIX_U:MIX_U + MIX]
        knew_ref[0] = mix_ref[tt - keep:, MIX_KC:MIX_KC + MIX]
        vnew_ref[0] = mix_ref[tt - keep:, MIX_VC:MIX_VC + MIX]


def _mixers(mix, gla, state, pw, l, n_seq, seq_len, pos0):
    has_state = state is not None
    tt = min(SEQ_TILE, seq_len)
    lg = min(CHUNK, seq_len)
    lq = min(CHUNK, seq_len)
    grp = _att_group(seq_len)
    nt = seq_len // tt
    assert has_state or tt == ATT_WIN
    kernel = functools.partial(_mixer_kernel, tt, lg, lq, grp, has_state, pos0)

    tile = lambda w: pl.BlockSpec((tt, w), lambda s, i: (s * nt + i, 0))
    in_specs = [tile(MIX_W), tile(GLA_W)]
    args = [mix, gla]
    if has_state:
        hist, s0t, kc, vc = state
        in_specs += [pl.BlockSpec((1, HIST_ROWS, MIX), lambda s, i: (s, 0, 0)),
                     pl.BlockSpec((1, 2, LANES, LANES), lambda s, i: (s, 0, 0, 0)),
                     pl.BlockSpec((1, ATT_WIN, MIX), lambda s, i: (s, 0, 0)),
                     pl.BlockSpec((1, ATT_WIN, MIX), lambda s, i: (s, 0, 0))]
        args += [hist, s0t, kc, vc]
    consts = [pw["w_gate2"], pw["b_gate"], pw["gla_norm_g"], pw["pool_map"], pw["pool_scale"],
              pw["bias_s"] if has_state else pw["bias_p"]]
    in_specs += [_layer_spec(c, l) for c in consts]
    args += consts
    keep = min(ATT_WIN, seq_len)
    per_seq = lambda r: pl.BlockSpec((1, r, MIX), lambda s, i: (s, 0, 0))

    return pl.pallas_call(
        kernel,
        grid=(n_seq, nt),
        in_specs=in_specs,
        out_specs=[pl.BlockSpec((tt, 3 * MIX), lambda s, i: (s * nt + i, 0)),
                   pl.BlockSpec((1, 2, LANES, LANES), lambda s, i: (s, 0, 0, 0)),
                   per_seq(HIST_ROWS), per_seq(keep), per_seq(keep)],
        out_shape=[jax.ShapeDtypeStruct((n_seq * seq_len, 3 * MIX), BF16),
                   jax.ShapeDtypeStruct((n_seq, 2, LANES, LANES), F32),
                   jax.ShapeDtypeStruct((n_seq, HIST_ROWS, MIX), F32),
                   jax.ShapeDtypeStruct((n_seq, keep, MIX), F32),
                   jax.ShapeDtypeStruct((n_seq, keep, MIX), F32)],
        scratch_shapes=[pltpu.VMEM((HIST_ROWS, MIX), F32),
                        pltpu.VMEM((2, LANES, LANES), F32),
                        pltpu.VMEM((ATT_WIN + tt, MIX), BF16),
                        pltpu.VMEM((ATT_WIN + tt, MIX), BF16),
                        pltpu.VMEM((tt, GLA_QK_W), F32)],
        compiler_params=pltpu.CompilerParams(dimension_semantics=("arbitrary", "arbitrary"),
                                             vmem_limit_bytes=VMEM_LIMIT),
        name="mixers_sample" if has_state else "mixers_prompt",
    )(*args)


def _merge_ffn_kernel(final, x_ref, ys_ref, hg_ref, wbr_ref, wout_ref, g2_ref, wf_ref, wo_ref, fg_ref, out_ref):
    tm = x_ref.shape[0]
    halves = [slice(h * (tm // MERGE_SPLIT), (h + 1) * (tm // MERGE_SPLIT)) for h in range(MERGE_SPLIT)]

    def merge(rows):
        merged = None
        for g in range(N_BRANCH):
            branch = jnp.dot(ys_ref[rows, g * MIX:(g + 1) * MIX], wbr_ref[g], preferred_element_type=F32)
            term = _sigmoid(hg_ref[rows, g * D_MODEL:(g + 1) * D_MODEL]) * branch
            merged = term if merged is None else merged + term
        return merged.astype(BF16)

    def residual(rows, merged):
        x1 = x_ref[rows, :] + jnp.dot(merged, wout_ref[...], preferred_element_type=F32)
        return x1, _rms(x1, g2_ref[...]).astype(BF16)

    def hidden(xn):
        a = jnp.dot(xn, wf_ref[:, :D_FF], preferred_element_type=F32)
        gt = jnp.dot(xn, wf_ref[:, D_FF:], preferred_element_type=F32)
        return (a * _sigmoid(a) * gt).astype(BF16)

    def finish(rows, x1, act):
        x2 = x1 + jnp.dot(act, wo_ref[...], preferred_element_type=F32)
        if final:
            x2 = _rms(x2, fg_ref[...])
        out_ref[rows, :] = x2

    merged = [merge(r) for r in halves]
    res = [residual(r, m) for r, m in zip(halves, merged)]
    acts = [hidden(xn) for _, xn in res]
    for r, (x1, _), act in zip(halves, res, acts):
        finish(r, x1, act)


def _merge_ffn(x, ys, gates, pw, l, final):
    rows = x.shape[0]
    tm = min(MERGE_SPLIT * ROW_TILE, rows)
    row_spec = lambda w: pl.BlockSpec((tm, w), lambda i: (i, 0))
    consts = [pw["w_branch"], pw["w_out"], pw["ffn_norm_g"], pw["w_ffn_in"], pw["w_ffn_out"]]
    return pl.pallas_call(
        functools.partial(_merge_ffn_kernel, final),
        grid=(rows // tm,),
        in_specs=([row_spec(D_MODEL), row_spec(3 * MIX), row_spec(GATE_W)] + [_layer_spec(c, l) for c in consts]
                  + [_layer_spec(pw["final_norm_g"], 0)]),
        out_specs=row_spec(D_MODEL),
        out_shape=jax.ShapeDtypeStruct((rows, D_MODEL), F32),
        compiler_params=pltpu.CompilerParams(dimension_semantics=("arbitrary",), vmem_limit_bytes=VMEM_LIMIT),
        name="merge_ffn",
    )(x, ys, gates, *consts, pw["final_norm_g"])


def _att_group(seq_len):
    return 2 if seq_len >= 2 * CHUNK else 1


def _band_bias(rel_bias, lq, grp):
    assert lq - 1 <= MAX_REL
    depth = rel_bias.shape[0]
    rtab = rel_bias[..., ::-1].astype(F32)
    n_far = ATT_WIN + lq - MAX_REL
    ext = jnp.concatenate([jnp.broadcast_to(rtab[..., :1], (depth, ATT_HEADS, n_far)),
                           rtab[..., 1:MAX_REL + lq]], axis=-1)
    wk = ATT_WIN + lq
    period = wk + lq - 1
    rot = jnp.concatenate([ext[..., lq - 1:], ext[..., :lq - 1]], axis=-1)
    chunk = jnp.tile(rot, (1, 1, lq))[..., :lq * (period - 1)].reshape(depth, ATT_HEADS, lq, period - 1)[..., :wk]
    blocks = [jnp.pad(chunk, ((0, 0), (0, 0), (0, 0), (j * lq, (grp - 1 - j) * lq)), constant_values=MASKED)
              for j in range(grp)]
    per_head = jnp.concatenate(blocks, axis=2)
    return per_head.reshape(depth, ATT_HEADS // 2, 2 * grp * lq, ATT_WIN + grp * lq)


IN_COLS = (0, 512, 768, 1024, 1536, 2048, 2064, 2576, 3088, 3600, 6672)


def _prep_params(attn_norm_g, w_in, w_gate2, b_gate, gla_norm_g, pool_map, pool_scale, rel_bias, w_branch, w_out,
                 ffn_norm_g, w_ffn_in, w_ffn_out, final_norm_g, prompt_len, sample_len):
    depth = w_in.shape[0]
    col = lambda a: w_in[:, :, IN_COLS[a]:IN_COLS[a + 1]]
    w_mix = jnp.concatenate([col(0), col(3), col(4), col(6), col(7), col(8)], axis=2)
    w_gla = jnp.concatenate([col(1), col(2), col(5),
                             jnp.zeros((depth, D_MODEL, LANES - GLA_RANK), w_in.dtype)], axis=2)
    wg2 = jnp.concatenate([w_gate2, jnp.zeros((depth, LANES - GLA_RANK, GLA_QK_W), w_gate2.dtype)], axis=1)
    vec = lambda a: a[:, None, :]
    return dict(
        attn_norm_g=vec(attn_norm_g), w_gates=col(9).astype(BF16), w_mix=w_mix.astype(BF16),
        w_gla=w_gla.astype(BF16), w_gate2=wg2.astype(BF16), b_gate=vec(b_gate),
        gla_norm_g=vec(gla_norm_g), pool_map=pool_map.astype(BF16), pool_scale=vec(pool_scale),
        bias_p=_band_bias(rel_bias, CHUNK, _att_group(prompt_len)),
        bias_s=_band_bias(rel_bias, min(CHUNK, sample_len), _att_group(sample_len)),
        w_branch=w_branch.astype(BF16), w_out=w_out.astype(BF16), ffn_norm_g=vec(ffn_norm_g),
        w_ffn_in=w_ffn_in.astype(BF16), w_ffn_out=w_ffn_out.astype(BF16),
        final_norm_g=final_norm_g[None, None, :])


def _layer(x, n_seq, seq_len, state, pw, l, pos0, final):
    gates, mix, gla = _inproj(x, pw, l)
    if state is not None:
        hist, s0, kc, vc = state
        hist = jnp.pad(hist, ((0, 0), (HIST_ROWS - POOL_HIST, 0), (0, 0)))
        s0t = s0.reshape(n_seq, 2, 2, GLA_DK, GLA_DV).transpose(0, 1, 4, 2, 3).reshape(n_seq, 2, LANES, LANES)
        state = (hist, s0t, kc.reshape(n_seq, ATT_WIN, MIX), vc.reshape(n_seq, ATT_WIN, MIX))
    ys, s_fin_t, pool16, k_new, v_new = _mixers(mix, gla, state, pw, l, n_seq, seq_len, pos0)
    x_new = _merge_ffn(x, ys, gates, pw, l, final)

    keep = min(ATT_WIN, seq_len)
    gla_new = (s_fin_t.reshape(n_seq, 2, GLA_DV, 2, GLA_DK).transpose(0, 1, 3, 4, 2)
               .reshape(n_seq, GLA_HEADS, GLA_DK, GLA_DV))
    return x_new, (pool16[:, HIST_ROWS - POOL_HIST:], gla_new,
                   k_new.reshape(n_seq, keep, ATT_HEADS, ATT_DH), v_new.reshape(n_seq, keep, ATT_HEADS, ATT_DH))


def kernel(x_prompt, x_sample, cache_pool, state_gla, cache_k, cache_v, attn_norm_g, w_in, w_gate2, b_gate,
           gla_norm_g, pool_map, pool_scale, rel_bias, w_branch, w_out, ffn_norm_g, w_ffn_in, w_ffn_out,
           final_norm_g):
    batch, seq, _ = x_prompt.shape
    dec_batch, dec_seq, _ = x_sample.shape
    depth = w_in.shape[0]
    hp = x_prompt.reshape(batch * seq, D_MODEL)
    hs = x_sample.reshape(dec_batch * dec_seq, D_MODEL)
    pw = _prep_params(attn_norm_g, w_in, w_gate2, b_gate, gla_norm_g, pool_map, pool_scale, rel_bias, w_branch,
                      w_out, ffn_norm_g, w_ffn_in, w_ffn_out, final_norm_g, seq, dec_seq)
    outs_p, outs_s = [], []
    for l in range(depth):
        final = l == depth - 1
        hp, sp = _layer(hp, batch, seq, None, pw, l, 0, final)
        hs, ss = _layer(hs, dec_batch, dec_seq, (cache_pool[l], state_gla[l], cache_k[l], cache_v[l]), pw, l,
                        PAST_LEN, final)
        outs_p.append(sp)
        outs_s.append(ss)
    stack = lambda outs, j: jnp.stack([o[j] for o in outs])
    return (hp.reshape(batch, seq, D_MODEL), hs.reshape(dec_batch, dec_seq, D_MODEL),
            stack(outs_p, 0), stack(outs_s, 0), stack(outs_p, 1), stack(outs_s, 1),
            stack(outs_p, 2), stack(outs_s, 2), stack(outs_p, 3), stack(outs_s, 3))
```
